```python
import jax, jax.numpy as jnp
from jax import lax
import numpy as np

D_MODEL = 1024
BATCH = 8
SEQ = 2048
DEPTH = 2

N_HEADS = 8
D_NOPE = 64
D_ROPE = 32
D_QK = D_NOPE + D_ROPE
D_V = 64
Q_LORA = 256
KV_LORA = 128
ATTN_WIDTH = N_HEADS * D_V
ROPE_THETA = 10000.0
Q_BLOCK = 128
CONV_WIDTH = 512
CONV_K = 3
D_FF = 2816
FFN_CONV_K = 3
PLE_DIM = 256
EPS = 1e-6
SPLITS = (
    Q_LORA,
    Q_LORA + KV_LORA,
    Q_LORA + KV_LORA + D_ROPE,
    Q_LORA + KV_LORA + D_ROPE + CONV_WIDTH,
    Q_LORA + KV_LORA + D_ROPE + 2 * CONV_WIDTH,
    Q_LORA + KV_LORA + D_ROPE + 3 * CONV_WIDTH,
)
IN_COLS = SPLITS[-1] + 2 * D_MODEL

kernel_name = "hybrid_mla_shortconv_convffn_ple_encoder"


def rmsnorm(x, g):
    xf = x.astype(jnp.float32)
    y = xf * lax.rsqrt(jnp.mean(xf * xf, axis=-1, keepdims=True) + EPS)
    return (y * g.astype(jnp.float32)).astype(x.dtype)


def dwconv(u, w):
    c = u.shape[-1]
    k = w.shape[0]
    pad = (k - 1) // 2
    return lax.conv_general_dilated(
        u, w[:, None, :].astype(u.dtype), window_strides=(1,), padding=[(pad, pad)],
        dimension_numbers=("NWC", "WIO", "NWC"), feature_group_count=c)


def rope_tables(positions):
    inv = ROPE_THETA ** (-jnp.arange(0, D_ROPE, 2, dtype=jnp.float32) / D_ROPE)
    ang = positions.astype(jnp.float32)[..., None] * inv
    return jnp.cos(ang)[:, :, None, :], jnp.sin(ang)[:, :, None, :]


def apply_rope(t, cos, sin):
    tf = t.astype(jnp.float32)
    t1, t2 = jnp.split(tf, 2, axis=-1)
    out = jnp.concatenate([t1 * cos - t2 * sin, t2 * cos + t1 * sin], axis=-1)
    return out.astype(t.dtype)


def split_norm(t, g):
    return jnp.concatenate([rmsnorm(t[..., :D_NOPE], g[:D_NOPE]),
                            rmsnorm(t[..., D_NOPE:], g[D_NOPE:])], axis=-1)


def block_attention(q, k, v):
    b, s, h, _ = q.shape
    nq = s // Q_BLOCK
    qb = q.reshape(b, nq, Q_BLOCK, h, D_QK).transpose(1, 0, 2, 3, 4)
    scale = D_QK ** -0.5

    def one_block(qi):
        sc = jnp.einsum("bqhd,bkhd->bhqk", qi, k).astype(jnp.float32) * scale
        pr = jax.nn.softmax(sc, axis=-1).astype(v.dtype)
        return jnp.einsum("bhqk,bkhd->bqhd", pr, v)

    o = lax.map(one_block, qb)
    return o.transpose(1, 0, 2, 3, 4).reshape(b, s, h * D_V)


def setup_inputs(seed: int = 0) -> dict:
    key = jax.random.key(seed)
    ks = jax.random.split(key, 24)
    f32 = jnp.float32

    def nrm(k, shape, fan_in, scale=1.0):
        return jax.random.normal(k, shape, f32) * (scale * fan_in ** -0.5)

    def gain(k, shape):
        return 1.0 + 0.02 * jax.random.normal(k, shape, f32)

    offsets = jax.random.randint(ks[2], (BATCH, 1), 0, 4096, dtype=jnp.int32)
    positions = (offsets + jnp.arange(SEQ, dtype=jnp.int32)[None, :]).astype(jnp.int32)
    return {
        "x": jax.random.normal(ks[0], (BATCH, SEQ, D_MODEL), f32),
        "p": jax.random.normal(ks[1], (DEPTH, BATCH, SEQ, PLE_DIM), f32),
        "positions": positions,
        "w_in": nrm(ks[3], (DEPTH, D_MODEL, IN_COLS), D_MODEL),
        "b_gate": 0.01 * jax.random.normal(ks[4], (DEPTH, 2 * D_MODEL), f32),
        "g_mix": gain(ks[5], (DEPTH, D_MODEL)),
        "g_q_lat": gain(ks[6], (DEPTH, Q_LORA)),
        "w_uq": nrm(ks[7], (DEPTH, Q_LORA, N_HEADS * D_QK), Q_LORA),
        "g_kv_lat": gain(ks[8], (DEPTH, KV_LORA)),
        "w_ukv": nrm(ks[9], (DEPTH, KV_LORA, N_HEADS * (D_NOPE + D_V)), KV_LORA),
        "g_q_head": gain(ks[10], (DEPTH, D_QK)),
        "g_k_head": gain(ks[11], (DEPTH, D_QK)),
        "w_attn_up": nrm(ks[12], (DEPTH, ATTN_WIDTH, D_MODEL), ATTN_WIDTH),
        "w_conv": nrm(ks[13], (DEPTH, CONV_K, CONV_WIDTH), CONV_K),
        "w_conv_up": nrm(ks[14], (DEPTH, CONV_WIDTH, D_MODEL), CONV_WIDTH),
        "w_o": nrm(ks[15], (DEPTH, D_MODEL, D_MODEL), D_MODEL, 0.5),
        "g_ffn": gain(ks[16], (DEPTH, D_MODEL)),
        "w_up": nrm(ks[17], (DEPTH, D_MODEL, 2 * D_FF), D_MODEL),
        "w_ffn_conv": nrm(ks[18], (DEPTH, FFN_CONV_K, 2 * D_FF), FFN_CONV_K),
        "w_down": nrm(ks[19], (DEPTH, D_FF, D_MODEL), D_FF, 0.5),
        "g_ple": gain(ks[20], (DEPTH, D_MODEL)),
        "w_ple_gate": nrm(ks[21], (DEPTH, D_MODEL, D_MODEL), D_MODEL),
        "w_ple": nrm(ks[22], (DEPTH, PLE_DIM, D_MODEL), PLE_DIM, 0.5),
    }


def reference(x, p, positions, w_in, b_gate, g_mix, g_q_lat, w_uq, g_kv_lat, w_ukv,
              g_q_head, g_k_head, w_attn_up, w_conv, w_conv_up, w_o, g_ffn, w_up,
              w_ffn_conv, w_down, g_ple, w_ple_gate, w_ple):
    b, s, _ = x.shape
    cos, sin = rope_tables(positions)
    for i in range(DEPTH):
        h = rmsnorm(x, g_mix[i])
        z = h @ w_in[i]
        c_q, c_kv, k_r, cb, cc, cx, gates = jnp.split(z, SPLITS, axis=-1)

        q = (rmsnorm(c_q, g_q_lat[i]) @ w_uq[i]).reshape(b, s, N_HEADS, D_QK)
        kv = (rmsnorm(c_kv, g_kv_lat[i]) @ w_ukv[i]).reshape(b, s, N_HEADS, D_NOPE + D_V)
        k_nope, v = kv[..., :D_NOPE], kv[..., D_NOPE:]
        k_rope = jnp.broadcast_to(k_r[:, :, None, :], (b, s, N_HEADS, D_ROPE))
        k = jnp.concatenate([k_nope, k_rope], axis=-1)
        q = split_norm(q, g_q_head[i])
        k = split_norm(k, g_k_head[i])
        q = jnp.concatenate([q[..., :D_NOPE], apply_rope(q[..., D_NOPE:], cos, sin)], axis=-1)
        k = jnp.concatenate([k[..., :D_NOPE], apply_rope(k[..., D_NOPE:], cos, sin)], axis=-1)
        y_attn = block_attention(q, k, v)

        y_conv = cb * dwconv(cc * cx, w_conv[i])

        g = jax.nn.sigmoid((gates + b_gate[i]).astype(jnp.float32)).astype(x.dtype)
        g_a, g_c = g[..., :D_MODEL], g[..., D_MODEL:]
        merged = g_a * (y_attn @ w_attn_up[i]) + g_c * (y_conv @ w_conv_up[i])
        x = x + merged @ w_o[i]

        h = rmsnorm(x, g_ffn[i])
        u = dwconv(h @ w_up[i], w_ffn_conv[i])
        a, val = u[..., :D_FF], u[..., D_FF:]
        x = x + (jax.nn.silu(a) * val) @ w_down[i]

        h = rmsnorm(x, g_ple[i])
        pg = jax.nn.sigmoid((h @ w_ple_gate[i]).astype(jnp.float32)).astype(x.dtype)
        x = x + pg * (p[i] @ w_ple[i])
    return x
```

```python
import functools

import numpy as np
import jax
import jax.numpy as jnp
from jax.experimental import pallas as pl
from jax.experimental.pallas import tpu as pltpu

D_MODEL = 1024
N_HEADS = 8
D_NOPE = 64
D_ROPE = 32
D_QK = D_NOPE + D_ROPE
D_V = 64
Q_LORA = 256
KV_LORA = 128
CONV_WIDTH = 512
D_FF = 2816
PLE_DIM = 256
EPS = 1e-6
ROPE_THETA = 10000.0

HEAD_PAD = 128
HALF_ROPE = D_ROPE // 2
LATENT_COLS = 512
HALO = 8
FF_CHUNK = 256
N_FF_CHUNKS = D_FF // FF_CHUNK
VMEM_LIMIT_BYTES = 56 * 1024 * 1024

ROW_TILE = 512
Q_TILE = 512
HEADS_PER_STEP = 2

BF16 = jnp.bfloat16
F32 = jnp.float32


def _dot(a, b):
    return jnp.dot(a, b, preferred_element_type=F32)


def _rms_rows(v, gain):
    return v * jax.lax.rsqrt(jnp.mean(v * v, axis=-1, keepdims=True) + EPS) * gain


def _rms_cols(v, gain):
    return v * jax.lax.rsqrt(jnp.mean(v * v, axis=0, keepdims=True) + EPS) * gain


def _rope_cols(t, cos, sin):
    t1, t2 = t[:HALF_ROPE], t[HALF_ROPE:]
    return t1 * cos - t2 * sin, t2 * cos + t1 * sin


def _const_spec(shape):
    zeros = (0,) * len(shape)
    return pl.BlockSpec(shape, lambda *_: zeros, pipeline_mode=pl.Buffered(1))


def _params(n_axes):
    return pltpu.CompilerParams(
        dimension_semantics=("arbitrary",) * n_axes,
        vmem_limit_bytes=VMEM_LIMIT_BYTES)


def _rope_table_kernel(pos_ref, inv_ref, cos_ref, sin_ref):
    ang = pos_ref[...].astype(F32) * inv_ref[...]
    cos_ref[...] = jnp.cos(ang)
    sin_ref[...] = jnp.sin(ang)


def _rope_tables(positions, n_tokens, seq):
    inv = ROPE_THETA ** (-np.arange(0, D_ROPE, 2, dtype=np.float32) / D_ROPE)
    inv = jnp.asarray(inv.astype(np.float32)).reshape(HALF_ROPE, 1)
    pos = positions.reshape(1, n_tokens)
    out = jax.ShapeDtypeStruct((HALF_ROPE, n_tokens), F32)
    return pl.pallas_call(
        _rope_table_kernel,
        grid=(n_tokens // seq,),
        in_specs=[pl.BlockSpec((1, seq), lambda i: (0, i)),
                  pl.BlockSpec((HALF_ROPE, 1), lambda i: (0, 0))],
        out_specs=[pl.BlockSpec((HALF_ROPE, seq), lambda i: (0, i))] * 2,
        out_shape=[out, out],
        compiler_params=_params(1),
        name="rope_tables",
    )(pos, inv)


def _inproj_kernel(x_ref, gmix_ref, wlat_ref, wconv_in_ref, gq_ref, wuq_ref, gkv_ref,
                   wukv_ref, gqh_ref, gkh_ref, cos_ref, sin_ref,
                   qt_ref, k_ref, vt_ref, cb_ref, ccx_ref):
    h = _rms_rows(x_ref[...], gmix_ref[...]).astype(BF16)

    zc = _dot(h, wconv_in_ref[...])
    cb_ref[...] = zc[:, :CONV_WIDTH].astype(BF16)
    ccx_ref[...] = (zc[:, CONV_WIDTH:2 * CONV_WIDTH] * zc[:, 2 * CONV_WIDTH:]).astype(BF16)

    zt = _dot(h, wlat_ref[...]).T
    cos = cos_ref[...]
    sin = sin_ref[...]

    cqn = _rms_cols(zt[:Q_LORA], gq_ref[...]).astype(BF16)
    qt = _dot(wuq_ref[...], cqn)
    gqh = gqh_ref[...]
    for hd in range(N_HEADS):
        blk = qt[hd * HEAD_PAD:(hd + 1) * HEAD_PAD]
        qn = _rms_cols(blk[:D_NOPE], gqh[:D_NOPE])
        qr = _rms_cols(blk[D_NOPE:D_QK], gqh[D_NOPE:])
        r1, r2 = _rope_cols(qr, cos, sin)
        full = jnp.concatenate([qn, r1, r2, blk[D_QK:]], axis=0)
        qt_ref[hd * HEAD_PAD:(hd + 1) * HEAD_PAD, :] = full.astype(BF16)

    ckvn = _rms_cols(zt[Q_LORA:Q_LORA + KV_LORA], gkv_ref[...]).astype(BF16)
    kvt = _dot(wukv_ref[...], ckvn)
    vt_ref[...] = kvt[N_HEADS * D_NOPE:].astype(BF16)

    gkh = gkh_ref[...]
    kr = _rms_cols(zt[Q_LORA + KV_LORA:Q_LORA + KV_LORA + D_ROPE], gkh[D_NOPE:])
    kr1, kr2 = _rope_cols(kr, cos, sin)
    pad = jnp.zeros((HEAD_PAD - D_QK, kr.shape[1]), F32)
    for hd in range(N_HEADS):
        kn = _rms_cols(kvt[hd * D_NOPE:(hd + 1) * D_NOPE], gkh[:D_NOPE])
        full = jnp.concatenate([kn, kr1, kr2, pad], axis=0)
        k_ref[:, hd * HEAD_PAD:(hd + 1) * HEAD_PAD] = full.T.astype(BF16)


def _inproj(x2d, tabs, w, seq):
    n_tokens = x2d.shape[0]
    tm = ROW_TILE
    cos_t, sin_t = tabs
    row = lambda width: pl.BlockSpec((tm, width), lambda i: (i, 0))
    col = lambda height: pl.BlockSpec((height, tm), lambda i: (0, i))
    outs = [
        jax.ShapeDtypeStruct((N_HEADS * HEAD_PAD, n_tokens), BF16),
        jax.ShapeDtypeStruct((n_tokens, N_HEADS * HEAD_PAD), BF16),
        jax.ShapeDtypeStruct((N_HEADS * D_V, n_tokens), BF16),
        jax.ShapeDtypeStruct((n_tokens, CONV_WIDTH), BF16),
        jax.ShapeDtypeStruct((n_tokens, CONV_WIDTH), BF16),
    ]
    return pl.pallas_call(
        _inproj_kernel,
        grid=(n_tokens // tm,),
        in_specs=[row(D_MODEL),
                  _const_spec((1, D_MODEL)),
                  _const_spec((D_MODEL, LATENT_COLS)),
                  _const_spec((D_MODEL, 3 * CONV_WIDTH)),
                  _const_spec((Q_LORA, 1)),
                  _const_spec((N_HEADS * HEAD_PAD, Q_LORA)),
                  _const_spec((KV_LORA, 1)),
                  _const_spec((N_HEADS * (D_NOPE + D_V), KV_LORA)),
                  _const_spec((D_QK, 1)),
                  _const_spec((D_QK, 1)),
                  col(HALF_ROPE), col(HALF_ROPE)],
        out_specs=[col(N_HEADS * HEAD_PAD), row(N_HEADS * HEAD_PAD), col(N_HEADS * D_V),
                   row(CONV_WIDTH), row(CONV_WIDTH)],
        out_shape=outs,
        compiler_params=_params(1),
        name="inproj",
    )(x2d, w["g_mix"], w["w_lat"], w["w_conv_in"], w["g_q_lat"], w["w_uq_t"], w["g_kv_lat"],
      w["w_ukv_t"], w["g_q_head"], w["g_k_head"], cos_t, sin_t)


def _attention_kernel(qt_ref, k_ref, vt_ref, yt_ref):
    for hd in range(HEADS_PER_STEP):
        k_h = k_ref[:, hd * HEAD_PAD:(hd + 1) * HEAD_PAD]
        q_h = qt_ref[hd * HEAD_PAD:(hd + 1) * HEAD_PAD, :]
        s = _dot(k_h, q_h)
        m = jnp.max(s, axis=0, keepdims=True)
        p = jnp.exp(s - m)
        l = jnp.sum(p, axis=0, keepdims=True)
        o = _dot(vt_ref[hd * D_V:(hd + 1) * D_V, :], p.astype(BF16))
        yt_ref[hd * D_V:(hd + 1) * D_V, :] = (o / l).astype(BF16)


def _attention(qt, k, vt, batch, seq):
    n_tokens = k.shape[0]
    tq = Q_TILE
    nq = seq // tq
    n_groups = N_HEADS // HEADS_PER_STEP
    return pl.pallas_call(
        _attention_kernel,
        grid=(batch, n_groups, nq),
        in_specs=[pl.BlockSpec((HEADS_PER_STEP * HEAD_PAD, tq), lambda b, g, i: (g, b * nq + i)),
                  pl.BlockSpec((seq, HEADS_PER_STEP * HEAD_PAD), lambda b, g, i: (b, g)),
                  pl.BlockSpec((HEADS_PER_STEP * D_V, seq), lambda b, g, i: (g, b))],
        out_specs=pl.BlockSpec((HEADS_PER_STEP * D_V, tq), lambda b, g, i: (g, b * nq + i)),
        out_shape=jax.ShapeDtypeStruct((N_HEADS * D_V, n_tokens), BF16),
        compiler_params=_params(3),
        name="attention",
    )(qt, k, vt)


def _conv3(buf_ref, w, n_rows):
    return (w[0:1] * buf_ref[pl.ds(HALO - 1, n_rows), :]
            + w[1:2] * buf_ref[pl.ds(HALO, n_rows), :]
            + w[2:3] * buf_ref[pl.ds(HALO + 1, n_rows), :])


def _edge_masks(tiles_per_seq):
    j = pl.program_id(0) % tiles_per_seq
    return (j != 0).astype(F32), (j != tiles_per_seq - 1).astype(F32)


def _merge_kernel(tiles_per_seq, x_ref, yt_ref, cb_ref, ccx_ref, ccx_prev_ref, ccx_next_ref,
                  gmix_ref, wgate_ref, bgate_ref, wconv_ref, wau_ref, wcu_ref, wo_ref,
                  out_ref, cs_ref):
    tm = x_ref.shape[0]
    keep_prev, keep_next = _edge_masks(tiles_per_seq)
    x = x_ref[...]
    h = _rms_rows(x, gmix_ref[...]).astype(BF16)

    two = 2 * HALO
    cs_ref[0:HALO, :] = ccx_prev_ref[...].astype(F32)[two - HALO:] * keep_prev
    cs_ref[HALO:HALO + tm, :] = ccx_ref[...].astype(F32)
    cs_ref[HALO + tm:, :] = ccx_next_ref[...].astype(F32)[:HALO] * keep_next
    y_conv = (cb_ref[...].astype(F32) * _conv3(cs_ref, wconv_ref[...], tm)).astype(BF16)

    y_attn = yt_ref[...].astype(F32).T.astype(BF16)
    gates = jax.nn.sigmoid(_dot(h, wgate_ref[...]) + bgate_ref[...])
    merged = (gates[:, :D_MODEL] * _dot(y_attn, wau_ref[...])
              + gates[:, D_MODEL:] * _dot(y_conv, wcu_ref[...]))
    out_ref[...] = x + _dot(merged.astype(BF16), wo_ref[...])


def _merge(x2d, yt, cb, ccx, w, seq):
    n_tokens = x2d.shape[0]
    tm = ROW_TILE
    two = 2 * HALO
    per_tile = tm // two
    last_blk = n_tokens // two - 1
    row = lambda width: pl.BlockSpec((tm, width), lambda i: (i, 0))
    return pl.pallas_call(
        functools.partial(_merge_kernel, seq // tm),
        grid=(n_tokens // tm,),
        in_specs=[row(D_MODEL),
                  pl.BlockSpec((N_HEADS * D_V, tm), lambda i: (0, i)),
                  row(CONV_WIDTH), row(CONV_WIDTH),
                  pl.BlockSpec((two, CONV_WIDTH), lambda i: (jnp.maximum(i * per_tile - 1, 0), 0)),
                  pl.BlockSpec((two, CONV_WIDTH),
                               lambda i: (jnp.minimum((i + 1) * per_tile, last_blk), 0)),
                  _const_spec((1, D_MODEL)),
                  _const_spec((D_MODEL, 2 * D_MODEL)),
                  _const_spec((1, 2 * D_MODEL)),
                  _const_spec((3, CONV_WIDTH)),
                  _const_spec((N_HEADS * D_V, D_MODEL)),
                  _const_spec((CONV_WIDTH, D_MODEL)),
                  _const_spec((D_MODEL, D_MODEL))],
        out_specs=row(D_MODEL),
        out_shape=jax.ShapeDtypeStruct((n_tokens, D_MODEL), F32),
        scratch_shapes=[pltpu.VMEM((tm + 2 * HALO, CONV_WIDTH), F32)],
        compiler_params=_params(1),
        name="merge",
    )(x2d, yt, cb, ccx, ccx, ccx, w["g_mix"], w["w_gate"], w["b_gate"], w["w_conv"],
      w["w_attn_up"], w["w_conv_up"], w["w_o"])


def _ffn_kernel(tiles_per_seq, x_ref, xprev_ref, xnext_ref, p_ref, gffn_ref, wup_ref, wfc_ref,
                wdn_ref, gple_ref, wpg_ref, wple_ref, out_ref, lhs_ref, ua_ref, uv_ref, acc_ref):
    tm = x_ref.shape[0]
    keep_prev, keep_next = _edge_masks(tiles_per_seq)
    x = x_ref[...]
    gffn = gffn_ref[...]
    lhs_ref[0:HALO, :] = (_rms_rows(xprev_ref[...], gffn) * keep_prev).astype(BF16)
    lhs_ref[HALO:HALO + tm, :] = _rms_rows(x, gffn).astype(BF16)
    lhs_ref[HALO + tm:, :] = (_rms_rows(xnext_ref[...], gffn) * keep_next).astype(BF16)
    acc_ref[...] = jnp.zeros_like(acc_ref)

    def chunk(j, carry):
        lhs = lhs_ref[...]
        ua_ref[...] = _dot(lhs, wup_ref[j])
        uv_ref[...] = _dot(lhs, wup_ref[N_FF_CHUNKS + j])
        a = _conv3(ua_ref, wfc_ref[j], tm)
        v = _conv3(uv_ref, wfc_ref[N_FF_CHUNKS + j], tm)
        act = (a * jax.nn.sigmoid(a) * v).astype(BF16)
        acc_ref[...] += _dot(act, wdn_ref[j])
        return carry

    jax.lax.fori_loop(0, N_FF_CHUNKS, chunk, 0)

    x2 = x + acc_ref[...]
    h3 = _rms_rows(x2, gple_ref[...]).astype(BF16)
    pg = jax.nn.sigmoid(_dot(h3, wpg_ref[...]))
    out_ref[...] = x2 + pg * _dot(p_ref[...].astype(BF16), wple_ref[...])


def _ffn(x2d, p2d, w, seq):
    n_tokens = x2d.shape[0]
    tm = ROW_TILE
    per_tile = tm // HALO
    last_blk = n_tokens // HALO - 1
    row = lambda width: pl.BlockSpec((tm, width), lambda i: (i, 0))
    return pl.pallas_call(
        functools.partial(_ffn_kernel, seq // tm),
        grid=(n_tokens // tm,),
        in_specs=[row(D_MODEL),
                  pl.BlockSpec((HALO, D_MODEL), lambda i: (jnp.maximum(i * per_tile - 1, 0), 0)),
                  pl.BlockSpec((HALO, D_MODEL),
                               lambda i: (jnp.minimum((i + 1) * per_tile, last_blk), 0)),
                  row(PLE_DIM),
                  _const_spec((1, D_MODEL)),
                  _const_spec((2 * N_FF_CHUNKS, D_MODEL, FF_CHUNK)),
                  _const_spec((2 * N_FF_CHUNKS, 3, FF_CHUNK)),
                  _const_spec((N_FF_CHUNKS, FF_CHUNK, D_MODEL)),
                  _const_spec((1, D_MODEL)),
                  _const_spec((D_MODEL, D_MODEL)),
                  _const_spec((PLE_DIM, D_MODEL))],
        out_specs=row(D_MODEL),
        out_shape=jax.ShapeDtypeStruct((n_tokens, D_MODEL), F32),
        scratch_shapes=[pltpu.VMEM((tm + 2 * HALO, D_MODEL), BF16),
                        pltpu.VMEM((tm + 2 * HALO, FF_CHUNK), F32),
                        pltpu.VMEM((tm + 2 * HALO, FF_CHUNK), F32),
                        pltpu.VMEM((tm, D_MODEL), F32)],
        compiler_params=_params(1),
        name="ffn",
    )(x2d, x2d, x2d, p2d, w["g_ffn"], w["w_up"], w["w_ffn_conv"], w["w_down"], w["g_ple"],
      w["w_ple_gate"], w["w_ple"])


def _prep_layer(i, w_in, b_gate, g_mix, g_q_lat, w_uq, g_kv_lat, w_ukv, g_q_head, g_k_head,
                w_attn_up, w_conv, w_conv_up, w_o, g_ffn, w_up, w_ffn_conv, w_down, g_ple,
                w_ple_gate, w_ple):
    n_lat = Q_LORA + KV_LORA + D_ROPE
    n_conv = 3 * CONV_WIDTH
    win = w_in[i]
    w_lat = jnp.pad(win[:, :n_lat], ((0, 0), (0, LATENT_COLS - n_lat))).astype(BF16)

    uq = w_uq[i].reshape(Q_LORA, N_HEADS, D_QK)
    uq = jnp.pad(uq, ((0, 0), (0, 0), (0, HEAD_PAD - D_QK)))
    w_uq_t = uq.reshape(Q_LORA, N_HEADS * HEAD_PAD).T.astype(BF16)

    ukv = w_ukv[i].reshape(KV_LORA, N_HEADS, D_NOPE + D_V)
    uk = ukv[:, :, :D_NOPE].reshape(KV_LORA, N_HEADS * D_NOPE)
    uv = ukv[:, :, D_NOPE:].reshape(KV_LORA, N_HEADS * D_V)
    w_ukv_t = jnp.concatenate([uk, uv], axis=1).T.astype(BF16)

    wup = w_up[i].reshape(D_MODEL, 2 * N_FF_CHUNKS, FF_CHUNK).transpose(1, 0, 2).astype(BF16)
    wfc = w_ffn_conv[i].reshape(3, 2 * N_FF_CHUNKS, FF_CHUNK).transpose(1, 0, 2)
    return {
        "g_mix": g_mix[i].reshape(1, D_MODEL),
        "w_lat": w_lat,
        "w_conv_in": win[:, n_lat:n_lat + n_conv].astype(BF16),
        "w_gate": win[:, n_lat + n_conv:].astype(BF16),
        "b_gate": b_gate[i].reshape(1, 2 * D_MODEL),
        "g_q_lat": g_q_lat[i].reshape(Q_LORA, 1),
        "w_uq_t": w_uq_t,
        "g_kv_lat": g_kv_lat[i].reshape(KV_LORA, 1),
        "w_ukv_t": w_ukv_t,
        "g_q_head": (g_q_head[i] * (D_QK ** -0.5)).reshape(D_QK, 1),
        "g_k_head": g_k_head[i].reshape(D_QK, 1),
        "w_attn_up": w_attn_up[i].astype(BF16),
        "w_conv": w_conv[i],
        "w_conv_up": w_conv_up[i].astype(BF16),
        "w_o": w_o[i].astype(BF16),
        "g_ffn": g_ffn[i].reshape(1, D_MODEL),
        "w_up": wup,
        "w_ffn_conv": wfc,
        "w_down": w_down[i].reshape(N_FF_CHUNKS, FF_CHUNK, D_MODEL).astype(BF16),
        "g_ple": g_ple[i].reshape(1, D_MODEL),
        "w_ple_gate": w_ple_gate[i].astype(BF16),
        "w_ple": w_ple[i].astype(BF16),
    }


def kernel(x, p, positions, w_in, b_gate, g_mix, g_q_lat, w_uq, g_kv_lat, w_ukv, g_q_head,
           g_k_head, w_attn_up, w_conv, w_conv_up, w_o, g_ffn, w_up, w_ffn_conv, w_down,
           g_ple, w_ple_gate, w_ple):
    batch, seq, d_model = x.shape
    depth = w_in.shape[0]
    assert d_model == D_MODEL and seq % ROW_TILE == 0 and seq % Q_TILE == 0
    n_tokens = batch * seq
    tabs = _rope_tables(positions, n_tokens, seq)
    x2d = x.reshape(n_tokens, D_MODEL)
    for i in range(depth):
        w = _prep_layer(i, w_in, b_gate, g_mix, g_q_lat, w_uq, g_kv_lat, w_ukv, g_q_head,
                        g_k_head, w_attn_up, w_conv, w_conv_up, w_o, g_ffn, w_up, w_ffn_conv,
                        w_down, g_ple, w_ple_gate, w_ple)
        qt, k, vt, cb, ccx = _inproj(x2d, tabs, w, seq)
        yt = _attention(qt, k, vt, batch, seq)
        x2d = _merge(x2d, yt, cb, ccx, w, seq)
        x2d = _ffn(x2d, p[i].reshape(n_tokens, PLE_DIM), w, seq)
    return x2d.reshape(batch, seq, D_MODEL)
```

```python
import functools

import numpy as np
import jax
import jax.numpy as jnp
from jax.experimental import pallas as pl
from jax.experimental.pallas import tpu as pltpu

D_MODEL = 1024
N_HEADS = 8
D_NOPE = 64
D_ROPE = 32
D_QK = D_NOPE + D_ROPE
D_V = 64
Q_LORA = 256
KV_LORA = 128
CONV_WIDTH = 512
D_FF = 2816
PLE_DIM = 256
EPS = 1e-6
ROPE_THETA = 10000.0

HEAD_PAD = 128
HALF_ROPE = D_ROPE // 2
V_ROWS = D_V + 16
LATENT_COLS = 512
HALO = 8
FF_CHUNK = 256
N_FF_CHUNKS = D_FF // FF_CHUNK
assert N_FF_CHUNKS % 2 == 1
VMEM_LIMIT_BYTES = 56 * 1024 * 1024

ROW_TILE = 512
Q_TILE = ROW_TILE
KEY_CHUNK = 512
HEADS_PER_STEP = 2
LOG2_E = 1.4426950408889634

BF16 = jnp.bfloat16
F32 = jnp.float32


def _dot(a, b):
    return jnp.dot(a, b, preferred_element_type=F32)


def _rms_rows(v, gain):
    return v * jax.lax.rsqrt(jnp.mean(v * v, axis=-1, keepdims=True) + EPS) * gain


def _rms_cols(v, gain):
    return v * jax.lax.rsqrt(jnp.mean(v * v, axis=0, keepdims=True) + EPS) * gain


def _rope_cols(t, cos, sin):
    t1, t2 = t[:HALF_ROPE], t[HALF_ROPE:]
    return t1 * cos - t2 * sin, t2 * cos + t1 * sin


def _const_spec(shape):
    zeros = (0,) * len(shape)
    return pl.BlockSpec(shape, lambda *_: zeros, pipeline_mode=pl.Buffered(1))


def _params(n_axes):
    return pltpu.CompilerParams(
        dimension_semantics=("arbitrary",) * n_axes,
        vmem_limit_bytes=VMEM_LIMIT_BYTES)


def _rope_table_kernel(pos_ref, inv_ref, cos_ref, sin_ref):
    ang = pos_ref[...].astype(F32) * inv_ref[...]
    cos_ref[...] = jnp.cos(ang)
    sin_ref[...] = jnp.sin(ang)


def _rope_tables(positions, n_tokens, seq):
    inv = ROPE_THETA ** (-np.arange(0, D_ROPE, 2, dtype=np.float32) / D_ROPE)
    inv = jnp.asarray(inv.astype(np.float32)).reshape(HALF_ROPE, 1)
    pos = positions.reshape(1, n_tokens)
    out = jax.ShapeDtypeStruct((HALF_ROPE, n_tokens), F32)
    return pl.pallas_call(
        _rope_table_kernel,
        grid=(n_tokens // seq,),
        in_specs=[pl.BlockSpec((1, seq), lambda i: (0, i)),
                  pl.BlockSpec((HALF_ROPE, 1), lambda i: (0, 0))],
        out_specs=[pl.BlockSpec((HALF_ROPE, seq), lambda i: (0, i))] * 2,
        out_shape=[out, out],
        compiler_params=_params(1),
        name="rope_tables",
    )(pos, inv)


def _inproj_kernel(x_ref, gmix_ref, wlat_ref, wconv_in_ref, gq_ref, wuq_ref, gkv_ref,
                   wukv_ref, gqh_ref, gkh_ref, cos_ref, sin_ref,
                   qt_ref, k_ref, vt_ref, cb_ref, ccx_ref):
    h = _rms_rows(x_ref[...], gmix_ref[...]).astype(BF16)

    zc = _dot(h, wconv_in_ref[...])
    cb_ref[...] = zc[:, :CONV_WIDTH].astype(BF16)
    ccx_ref[...] = (zc[:, CONV_WIDTH:2 * CONV_WIDTH] * zc[:, 2 * CONV_WIDTH:]).astype(BF16)

    zt = _dot(h, wlat_ref[...]).T
    cos = cos_ref[...]
    sin = sin_ref[...]

    cqn = _rms_cols(zt[:Q_LORA], gq_ref[...]).astype(BF16)
    qt = _dot(wuq_ref[...], cqn)
    gqh = gqh_ref[...]
    for hd in range(N_HEADS):
        blk = qt[hd * HEAD_PAD:(hd + 1) * HEAD_PAD]
        qn = _rms_cols(blk[:D_NOPE], gqh[:D_NOPE])
        qr = _rms_cols(blk[D_NOPE:D_QK], gqh[D_NOPE:])
        r1, r2 = _rope_cols(qr, cos, sin)
        full = jnp.concatenate([qn, r1, r2, blk[D_QK:]], axis=0)
        qt_ref[0, hd] = full.astype(BF16)

    ckvn = _rms_cols(zt[Q_LORA:Q_LORA + KV_LORA], gkv_ref[...]).astype(BF16)
    kvt = _dot(wukv_ref[...], ckvn)
    for hd in range(N_HEADS):
        lo = N_HEADS * D_NOPE + hd * D_V
        vt_ref[hd, 0:D_V, :] = kvt[lo:lo + D_V].astype(BF16)
        vt_ref[hd, D_V:, :] = jnp.ones((V_ROWS - D_V, kvt.shape[1]), BF16)

    gkh = gkh_ref[...]
    kr = _rms_cols(zt[Q_LORA + KV_LORA:Q_LORA + KV_LORA + D_ROPE], gkh[D_NOPE:])
    kr1, kr2 = _rope_cols(kr, cos, sin)
    pad = jnp.zeros((HEAD_PAD - D_QK, kr.shape[1]), F32)
    for hd in range(N_HEADS):
        kn = _rms_cols(kvt[hd * D_NOPE:(hd + 1) * D_NOPE], gkh[:D_NOPE])
        full = jnp.concatenate([kn, kr1, kr2, pad], axis=0)
        k_ref[hd] = full.T.astype(BF16)


def _inproj(x2d, tabs, w, seq):
    n_tokens = x2d.shape[0]
    tm = ROW_TILE
    cos_t, sin_t = tabs
    row = lambda width: pl.BlockSpec((tm, width), lambda i: (i, 0))
    col = lambda height: pl.BlockSpec((height, tm), lambda i: (0, i))
    outs = [
        jax.ShapeDtypeStruct((n_tokens // tm, N_HEADS, HEAD_PAD, tm), BF16),
        jax.ShapeDtypeStruct((N_HEADS, n_tokens, HEAD_PAD), BF16),
        jax.ShapeDtypeStruct((N_HEADS, V_ROWS, n_tokens), BF16),
        jax.ShapeDtypeStruct((n_tokens, CONV_WIDTH), BF16),
        jax.ShapeDtypeStruct((n_tokens, CONV_WIDTH), BF16),
    ]
    return pl.pallas_call(
        _inproj_kernel,
        grid=(n_tokens // tm,),
        in_specs=[row(D_MODEL),
                  _const_spec((1, D_MODEL)),
                  _const_spec((D_MODEL, LATENT_COLS)),
                  _const_spec((D_MODEL, 3 * CONV_WIDTH)),
                  _const_spec((Q_LORA, 1)),
                  _const_spec((N_HEADS * HEAD_PAD, Q_LORA)),
                  _const_spec((KV_LORA, 1)),
                  _const_spec((N_HEADS * (D_NOPE + D_V), KV_LORA)),
                  _const_spec((D_QK, 1)),
                  _const_spec((D_QK, 1)),
                  col(HALF_ROPE), col(HALF_ROPE)],
        out_specs=[pl.BlockSpec((1, N_HEADS, HEAD_PAD, tm), lambda i: (i, 0, 0, 0)),
                   pl.BlockSpec((N_HEADS, tm, HEAD_PAD), lambda i: (0, i, 0)),
                   pl.BlockSpec((N_HEADS, V_ROWS, tm), lambda i: (0, 0, i)),
                   row(CONV_WIDTH), row(CONV_WIDTH)],
        out_shape=outs,
        compiler_params=_params(1),
        name="inproj",
    )(x2d, w["g_mix"], w["w_lat"], w["w_conv_in"], w["g_q_lat"], w["w_uq_t"], w["g_kv_lat"],
      w["w_ukv_t"], w["g_q_head"], w["g_k_head"], cos_t, sin_t)


def _attention_kernel(qt_ref, k_ref, vt_ref, yt_ref, s0_ref, s1_ref):
    nq = qt_ref.shape[0]
    s_refs = (s0_ref, s1_ref)

    def scores(qi, hd):
        s = _dot(k_ref[hd], qt_ref[qi, hd])
        s_refs[hd][...] = s
        return jnp.max(s, axis=0, keepdims=True)

    def values(qi, hd, m):
        p = jnp.exp2(s_refs[hd][...] - m).astype(BF16)
        o = _dot(vt_ref[hd], p)
        l = o[D_V:D_V + 1]
        yt_ref[qi, hd] = (o[:D_V] * (1.0 / l)).astype(BF16)

    def step(qi, m0):
        m1 = scores(qi, 1)
        values(qi, 0, m0)
        m0_next = scores(qi + 1, 0)
        values(qi, 1, m1)
        return m0_next

    m0 = jax.lax.fori_loop(0, nq - 1, step, scores(0, 0))
    m1 = scores(nq - 1, 1)
    values(nq - 1, 0, m0)
    values(nq - 1, 1, m1)


def _attention(qt, k, vt, batch, seq):
    tq = Q_TILE
    nq = seq // tq
    n_groups = N_HEADS // HEADS_PER_STEP
    return pl.pallas_call(
        _attention_kernel,
        grid=(batch, n_groups),
        in_specs=[pl.BlockSpec((nq, HEADS_PER_STEP, HEAD_PAD, tq), lambda b, g: (b, g, 0, 0)),
                  pl.BlockSpec((HEADS_PER_STEP, seq, HEAD_PAD), lambda b, g: (g, b, 0)),
                  pl.BlockSpec((HEADS_PER_STEP, V_ROWS, seq), lambda b, g: (g, 0, b))],
        out_specs=pl.BlockSpec((nq, HEADS_PER_STEP, D_V, tq), lambda b, g: (b, g, 0, 0)),
        out_shape=jax.ShapeDtypeStruct((batch * nq, N_HEADS, D_V, tq), BF16),
        scratch_shapes=[pltpu.VMEM((seq, tq), F32)] * HEADS_PER_STEP,
        compiler_params=_params(2),
        name="attention",
    )(qt, k, vt)


def _conv3(buf_ref, w, n_rows):
    return (w[0:1] * buf_ref[pl.ds(HALO - 1, n_rows), :]
            + w[1:2] * buf_ref[pl.ds(HALO, n_rows), :]
            + w[2:3] * buf_ref[pl.ds(HALO + 1, n_rows), :])


def _edge_masks(tiles_per_seq):
    j = pl.program_id(0) % tiles_per_seq
    return (j != 0).astype(F32), (j != tiles_per_seq - 1).astype(F32)


def _merge_kernel(tiles_per_seq, x_ref, yt_ref, cb_ref, ccx_ref, ccx_prev_ref, ccx_next_ref,
                  gmix_ref, wgate_ref, bgate_ref, wconv_ref, wau_ref, wcu_ref, wo_ref,
                  out_ref, cs_ref):
    tm = x_ref.shape[0]
    keep_prev, keep_next = _edge_masks(tiles_per_seq)
    x = x_ref[...]
    h = _rms_rows(x, gmix_ref[...]).astype(BF16)

    two = 2 * HALO
    cs_ref[0:HALO, :] = ccx_prev_ref[...].astype(F32)[two - HALO:] * keep_prev
    cs_ref[HALO:HALO + tm, :] = ccx_ref[...].astype(F32)
    cs_ref[HALO + tm:, :] = ccx_next_ref[...].astype(F32)[:HALO] * keep_next
    y_conv = (cb_ref[...].astype(F32) * _conv3(cs_ref, wconv_ref[...], tm)).astype(BF16)

    yt = yt_ref[0].reshape(N_HEADS * D_V, tm)
    y_attn = yt.astype(F32).T.astype(BF16)
    gates = jax.nn.sigmoid(_dot(h, wgate_ref[...]) + bgate_ref[...])
    merged = (gates[:, :D_MODEL] * _dot(y_attn, wau_ref[...])
              + gates[:, D_MODEL:] * _dot(y_conv, wcu_ref[...]))
    out_ref[...] = x + _dot(merged.astype(BF16), wo_ref[...])


def _merge(x2d, yt, cb, ccx, w, seq):
    n_tokens = x2d.shape[0]
    tm = ROW_TILE
    two = 2 * HALO
    per_tile = tm // two
    last_blk = n_tokens // two - 1
    row = lambda width: pl.BlockSpec((tm, width), lambda i: (i, 0))
    return pl.pallas_call(
        functools.partial(_merge_kernel, seq // tm),
        grid=(n_tokens // tm,),
        in_specs=[row(D_MODEL),
                  pl.BlockSpec((1, N_HEADS, D_V, tm), lambda i: (i, 0, 0, 0)),
                  row(CONV_WIDTH), row(CONV_WIDTH),
                  pl.BlockSpec((two, CONV_WIDTH), lambda i: (jnp.maximum(i * per_tile - 1, 0), 0)),
                  pl.BlockSpec((two, CONV_WIDTH),
                               lambda i: (jnp.minimum((i + 1) * per_tile, last_blk), 0)),
                  _const_spec((1, D_MODEL)),
                  _const_spec((D_MODEL, 2 * D_MODEL)),
                  _const_spec((1, 2 * D_MODEL)),
                  _const_spec((3, CONV_WIDTH)),
                  _const_spec((N_HEADS * D_V, D_MODEL)),
                  _const_spec((CONV_WIDTH, D_MODEL)),
                  _const_spec((D_MODEL, D_MODEL))],
        out_specs=row(D_MODEL),
        out_shape=jax.ShapeDtypeStruct((n_tokens, D_MODEL), F32),
        scratch_shapes=[pltpu.VMEM((tm + 2 * HALO, CONV_WIDTH), F32)],
        compiler_params=_params(1),
        name="merge",
    )(x2d, yt, cb, ccx, ccx, ccx, w["g_mix"], w["w_gate"], w["b_gate"], w["w_conv"],
      w["w_attn_up"], w["w_conv_up"], w["w_o"])


def _ffn_kernel(tiles_per_seq, x_ref, xprev_ref, xnext_ref, p_ref, gffn_ref, wup_ref, wfc_ref,
                wdn_ref, gple_ref, wpg_ref, wple_ref, out_ref, lhs_ref, ua0_ref, uv0_ref,
                ua1_ref, uv1_ref, acc_ref):
    tm = x_ref.shape[0]
    keep_prev, keep_next = _edge_masks(tiles_per_seq)
    x = x_ref[...]
    gffn = gffn_ref[...]
    lhs_ref[...] = jnp.concatenate(
        [_rms_rows(xprev_ref[...], gffn) * keep_prev,
         _rms_rows(x, gffn),
         _rms_rows(xnext_ref[...], gffn) * keep_next], axis=0).astype(BF16)
    acc_ref[...] = jnp.zeros_like(acc_ref)
    bufs = ((ua0_ref, uv0_ref), (ua1_ref, uv1_ref))

    def up(j, slot):
        lhs = lhs_ref[...]
        ua_ref, uv_ref = bufs[slot]
        ua_ref[...] = _dot(lhs, wup_ref[j])
        uv_ref[...] = _dot(lhs, wup_ref[N_FF_CHUNKS + j])

    def down(j, slot):
        ua_ref, uv_ref = bufs[slot]
        a = _conv3(ua_ref, wfc_ref[j], tm)
        v = _conv3(uv_ref, wfc_ref[N_FF_CHUNKS + j], tm)
        act = (a * jax.nn.sigmoid(a) * v).astype(BF16)
        acc_ref[...] += _dot(act, wdn_ref[j])

    def pair(i, carry):
        j = 2 * i
        up(j + 1, 1)
        down(j, 0)
        up(j + 2, 0)
        down(j + 1, 1)
        return carry

    up(0, 0)
    jax.lax.fori_loop(0, (N_FF_CHUNKS - 1) // 2, pair, 0)
    down(N_FF_CHUNKS - 1, 0)

    x2 = x + acc_ref[...]
    h3 = _rms_rows(x2, gple_ref[...]).astype(BF16)
    pg = jax.nn.sigmoid(_dot(h3, wpg_ref[...]))
    out_ref[...] = x2 + pg * _dot(p_ref[...].astype(BF16), wple_ref[...])


def _ffn(x2d, p2d, w, seq):
    n_tokens = x2d.shape[0]
    tm = ROW_TILE
    per_tile = tm // HALO
    last_blk = n_tokens // HALO - 1
    row = lambda width: pl.BlockSpec((tm, width), lambda i: (i, 0))
    return pl.pallas_call(
        functools.partial(_ffn_kernel, seq // tm),
        grid=(n_tokens // tm,),
        in_specs=[row(D_MODEL),
                  pl.BlockSpec((HALO, D_MODEL), lambda i: (jnp.maximum(i * per_tile - 1, 0), 0)),
                  pl.BlockSpec((HALO, D_MODEL),
                               lambda i: (jnp.minimum((i + 1) * per_tile, last_blk), 0)),
                  row(PLE_DIM),
                  _const_spec((1, D_MODEL)),
                  _const_spec((2 * N_FF_CHUNKS, D_MODEL, FF_CHUNK)),
                  _const_spec((2 * N_FF_CHUNKS, 3, FF_CHUNK)),
                  _const_spec((N_FF_CHUNKS, FF_CHUNK, D_MODEL)),
                  _const_spec((1, D_MODEL)),
                  _const_spec((D_MODEL, D_MODEL)),
                  _const_spec((PLE_DIM, D_MODEL))],
        out_specs=row(D_MODEL),
        out_shape=jax.ShapeDtypeStruct((n_tokens, D_MODEL), F32),
        scratch_shapes=[pltpu.VMEM((tm + 2 * HALO, D_MODEL), BF16)]
        + [pltpu.VMEM((tm + 2 * HALO, FF_CHUNK), F32)] * 4
        + [pltpu.VMEM((tm, D_MODEL), F32)],
        compiler_params=_params(1),
        name="ffn",
    )(x2d, x2d, x2d, p2d, w["g_ffn"], w["w_up"], w["w_ffn_conv"], w["w_down"], w["g_ple"],
      w["w_ple_gate"], w["w_ple"])


def _prep_layer(i, w_in, b_gate, g_mix, g_q_lat, w_uq, g_kv_lat, w_ukv, g_q_head, g_k_head,
                w_attn_up, w_conv, w_conv_up, w_o, g_ffn, w_up, w_ffn_conv, w_down, g_ple,
                w_ple_gate, w_ple):
    n_lat = Q_LORA + KV_LORA + D_ROPE
    n_conv = 3 * CONV_WIDTH
    win = w_in[i]
    w_lat = jnp.pad(win[:, :n_lat], ((0, 0), (0, LATENT_COLS - n_lat))).astype(BF16)

    uq = w_uq[i].reshape(Q_LORA, N_HEADS, D_QK)
    uq = jnp.pad(uq, ((0, 0), (0, 0), (0, HEAD_PAD - D_QK)))
    w_uq_t = uq.reshape(Q_LORA, N_HEADS * HEAD_PAD).T.astype(BF16)

    ukv = w_ukv[i].reshape(KV_LORA, N_HEADS, D_NOPE + D_V)
    uk = ukv[:, :, :D_NOPE].reshape(KV_LORA, N_HEADS * D_NOPE)
    uv = ukv[:, :, D_NOPE:].reshape(KV_LORA, N_HEADS * D_V)
    w_ukv_t = jnp.concatenate([uk, uv], axis=1).T.astype(BF16)

    wup = w_up[i].reshape(D_MODEL, 2 * N_FF_CHUNKS, FF_CHUNK).transpose(1, 0, 2).astype(BF16)
    wfc = w_ffn_conv[i].reshape(3, 2 * N_FF_CHUNKS, FF_CHUNK).transpose(1, 0, 2)
    return {
        "g_mix": g_mix[i].reshape(1, D_MODEL),
        "w_lat": w_lat,
        "w_conv_in": win[:, n_lat:n_lat + n_conv].astype(BF16),
        "w_gate": win[:, n_lat + n_conv:].astype(BF16),
        "b_gate": b_gate[i].reshape(1, 2 * D_MODEL),
        "g_q_lat": g_q_lat[i].reshape(Q_LORA, 1),
        "w_uq_t": w_uq_t,
        "g_kv_lat": g_kv_lat[i].reshape(KV_LORA, 1),
        "w_ukv_t": w_ukv_t,
        "g_q_head": (g_q_head[i] * (D_QK ** -0.5 * LOG2_E)).reshape(D_QK, 1),
        "g_k_head": g_k_head[i].reshape(D_QK, 1),
        "w_attn_up": w_attn_up[i].astype(BF16),
        "w_conv": w_conv[i],
        "w_conv_up": w_conv_up[i].astype(BF16),
        "w_o": w_o[i].astype(BF16),
        "g_ffn": g_ffn[i].reshape(1, D_MODEL),
        "w_up": wup,
        "w_ffn_conv": wfc,
        "w_down": w_down[i].reshape(N_FF_CHUNKS, FF_CHUNK, D_MODEL).astype(BF16),
        "g_ple": g_ple[i].reshape(1, D_MODEL),
        "w_ple_gate": w_ple_gate[i].astype(BF16),
        "w_ple": w_ple[i].astype(BF16),
    }


def kernel(x, p, positions, w_in, b_gate, g_mix, g_q_lat, w_uq, g_kv_lat, w_ukv, g_q_head,
           g_k_head, w_attn_up, w_conv, w_conv_up, w_o, g_ffn, w_up, w_ffn_conv, w_down,
           g_ple, w_ple_gate, w_ple):
    batch, seq, d_model = x.shape
    depth = w_in.shape[0]
    assert d_model == D_MODEL and seq % ROW_TILE == 0 and seq % Q_TILE == 0
    n_tokens = batch * seq
    tabs = _rope_tables(positions, n_tokens, seq)
    x2d = x.reshape(n_tokens, D_MODEL)
    for i in range(depth):
        w = _prep_layer(i, w_in, b_gate, g_mix, g_q_lat, w_uq, g_kv_lat, w_ukv, g_q_head,
                        g_k_head, w_attn_up, w_conv, w_conv_up, w_o, g_ffn, w_up, w_ffn_conv,
                        w_down, g_ple, w_ple_gate, w_ple)
        qt, k, vt, cb, ccx = _inproj(x2d, tabs, w, seq)
        yt = _attention(qt, k, vt, batch, seq)
        x2d = _merge(x2d, yt, cb, ccx, w, seq)
        x2d = _ffn(x2d, p[i].reshape(n_tokens, PLE_DIM), w, seq)
    return x2d.reshape(batch, seq, D_MODEL)
```

```python
import functools

import numpy as np
import jax
import jax.numpy as jnp
from jax.experimental import pallas as pl
from jax.experimental.pallas import tpu as pltpu

D_MODEL = 1024
N_HEADS = 8
D_NOPE = 64
D_ROPE = 32
D_QK = D_NOPE + D_ROPE
D_V = 64
Q_LORA = 256
KV_LORA = 128
CONV_WIDTH = 512
D_FF = 2816
PLE_DIM = 256
EPS = 1e-6
ROPE_THETA = 10000.0

HEAD_PAD = 128
HALF_ROPE = D_ROPE // 2
V_ROWS = D_V + 16
N_LATENT = Q_LORA + KV_LORA + D_ROPE
LATENT_COLS = 512
N_CONV_IN = 3 * CONV_WIDTH
N_GATES = 2 * D_MODEL
HALO = 8
FF_CHUNK = 256
N_FF_CHUNKS = D_FF // FF_CHUNK
DOWN_SPLITS = (N_FF_CHUNKS,)
VMEM_LIMIT_BYTES = 56 * 1024 * 1024

ROW_TILE = 512
Q_TILE = ROW_TILE
HEADS_PER_STEP = 2
LOG2_E = 1.4426950408889634

BF16 = jnp.bfloat16
F32 = jnp.float32


def _dot(a, b):
    return jnp.dot(a, b, preferred_element_type=F32)


def _rms_rows(v, gain):
    return v * jax.lax.rsqrt(jnp.mean(v * v, axis=-1, keepdims=True) + EPS) * gain


def _rms_cols(v, gain):
    return v * jax.lax.rsqrt(jnp.mean(v * v, axis=0, keepdims=True) + EPS) * gain


def _rope_cols(t, cos, sin):
    t1, t2 = t[:HALF_ROPE], t[HALF_ROPE:]
    return t1 * cos - t2 * sin, t2 * cos + t1 * sin


def _layer_spec(layer, rows, cols, col_block=0):
    return pl.BlockSpec((None, rows, cols), lambda *_: (layer, 0, col_block),
                        pipeline_mode=pl.Buffered(1))


def _params(n_axes):
    return pltpu.CompilerParams(
        dimension_semantics=("arbitrary",) * n_axes,
        vmem_limit_bytes=VMEM_LIMIT_BYTES)


def _rope_table_kernel(pos_ref, inv_ref, cos_ref, sin_ref):
    ang = pos_ref[...].astype(F32) * inv_ref[...]
    cos_ref[...] = jnp.cos(ang)
    sin_ref[...] = jnp.sin(ang)


def _rope_tables(positions, n_tokens, seq):
    inv = ROPE_THETA ** (-np.arange(0, D_ROPE, 2, dtype=np.float32) / D_ROPE)
    inv = jnp.asarray(inv.astype(np.float32)).reshape(HALF_ROPE, 1)
    pos = positions.reshape(1, n_tokens)
    out = jax.ShapeDtypeStruct((HALF_ROPE, n_tokens), F32)
    return pl.pallas_call(
        _rope_table_kernel,
        grid=(n_tokens // seq,),
        in_specs=[pl.BlockSpec((1, seq), lambda i: (0, i)),
                  pl.BlockSpec((HALF_ROPE, 1), lambda i: (0, 0))],
        out_specs=[pl.BlockSpec((HALF_ROPE, seq), lambda i: (0, i))] * 2,
        out_shape=[out, out],
        compiler_params=_params(1),
        name="rope_tables",
    )(pos, inv)


def _inproj_kernel(x_ref, gmix_ref, wconv_in_ref, wlat_ref, gq_ref, wuq_ref, gkv_ref,
                   wukv_ref, gqh_ref, gkh_ref, cos_ref, sin_ref,
                   qt_ref, k_ref, vt_ref, cb_ref, ccx_ref):
    h = _rms_rows(x_ref[...], gmix_ref[...]).astype(BF16)

    zc = _dot(h, wconv_in_ref[...])
    cb_ref[...] = zc[:, :CONV_WIDTH].astype(BF16)
    ccx_ref[...] = (zc[:, CONV_WIDTH:2 * CONV_WIDTH] * zc[:, 2 * CONV_WIDTH:]).astype(BF16)

    zt = _dot(h, wlat_ref[...]).T
    cos = cos_ref[...]
    sin = sin_ref[...]

    cqn = _rms_cols(zt[:Q_LORA], gq_ref[...]).astype(BF16)
    qt = _dot(wuq_ref[...], cqn)
    gqh = gqh_ref[...]
    for hd in range(N_HEADS):
        blk = qt[hd * HEAD_PAD:(hd + 1) * HEAD_PAD]
        qn = _rms_cols(blk[:D_NOPE], gqh[:D_NOPE])
        qr = _rms_cols(blk[D_NOPE:D_QK], gqh[D_NOPE:])
        r1, r2 = _rope_cols(qr, cos, sin)
        full = jnp.concatenate([qn, r1, r2, blk[D_QK:]], axis=0)
        qt_ref[0, hd] = full.astype(BF16)

    ckvn = _rms_cols(zt[Q_LORA:Q_LORA + KV_LORA], gkv_ref[...]).astype(BF16)
    kvt = _dot(wukv_ref[...], ckvn)
    for hd in range(N_HEADS):
        lo = N_HEADS * D_NOPE + hd * D_V
        vt_ref[hd, 0:D_V, :] = kvt[lo:lo + D_V].astype(BF16)
        vt_ref[hd, D_V:, :] = jnp.ones((V_ROWS - D_V, kvt.shape[1]), BF16)

    gkh = gkh_ref[...]
    kr = _rms_cols(zt[Q_LORA + KV_LORA:N_LATENT], gkh[D_NOPE:])
    kr1, kr2 = _rope_cols(kr, cos, sin)
    pad = jnp.zeros((HEAD_PAD - D_QK, kr.shape[1]), F32)
    for hd in range(N_HEADS):
        kn = _rms_cols(kvt[hd * D_NOPE:(hd + 1) * D_NOPE], gkh[:D_NOPE])
        full = jnp.concatenate([kn, kr1, kr2, pad], axis=0)
        k_ref[hd] = full.T.astype(BF16)


def _inproj(layer, x2d, tabs, w):
    n_tokens = x2d.shape[0]
    tm = ROW_TILE
    cos_t, sin_t = tabs
    row = lambda width: pl.BlockSpec((tm, width), lambda i: (i, 0))
    col = lambda height: pl.BlockSpec((height, tm), lambda i: (0, i))
    outs = [
        jax.ShapeDtypeStruct((n_tokens // tm, N_HEADS, HEAD_PAD, tm), BF16),
        jax.ShapeDtypeStruct((N_HEADS, n_tokens, HEAD_PAD), BF16),
        jax.ShapeDtypeStruct((N_HEADS, V_ROWS, n_tokens), BF16),
        jax.ShapeDtypeStruct((n_tokens, CONV_WIDTH), BF16),
        jax.ShapeDtypeStruct((n_tokens, CONV_WIDTH), BF16),
    ]
    lat_block = N_CONV_IN // LATENT_COLS
    return pl.pallas_call(
        _inproj_kernel,
        grid=(n_tokens // tm,),
        in_specs=[row(D_MODEL),
                  _layer_spec(layer, 1, D_MODEL),
                  _layer_spec(layer, D_MODEL, N_CONV_IN),
                  _layer_spec(layer, D_MODEL, LATENT_COLS, lat_block),
                  _layer_spec(layer, Q_LORA, 1),
                  _layer_spec(layer, N_HEADS * HEAD_PAD, Q_LORA),
                  _layer_spec(layer, KV_LORA, 1),
                  _layer_spec(layer, N_HEADS * (D_NOPE + D_V), KV_LORA),
                  _layer_spec(layer, D_QK, 1),
                  _layer_spec(layer, D_QK, 1),
                  col(HALF_ROPE), col(HALF_ROPE)],
        out_specs=[pl.BlockSpec((1, N_HEADS, HEAD_PAD, tm), lambda i: (i, 0, 0, 0)),
                   pl.BlockSpec((N_HEADS, tm, HEAD_PAD), lambda i: (0, i, 0)),
                   pl.BlockSpec((N_HEADS, V_ROWS, tm), lambda i: (0, 0, i)),
                   row(CONV_WIDTH), row(CONV_WIDTH)],
        out_shape=outs,
        compiler_params=_params(1),
        name="inproj",
    )(x2d, w["g_mix"], w["w_in"], w["w_in"], w["g_q_lat"], w["w_uq_t"], w["g_kv_lat"],
      w["w_ukv_t"], w["g_q_head"], w["g_k_head"], cos_t, sin_t)


def _attention_kernel(qt_ref, k_ref, vt_ref, yt_ref, s0_ref, s1_ref):
    nq = qt_ref.shape[0]
    s_refs = (s0_ref, s1_ref)

    def scores(qi, hd):
        s = _dot(k_ref[hd], qt_ref[qi, hd])
        s_refs[hd][...] = s
        return jnp.max(s, axis=0, keepdims=True)

    def values(qi, hd, m):
        p = jnp.exp2(s_refs[hd][...] - m).astype(BF16)
        o = _dot(vt_ref[hd], p)
        l = o[D_V:D_V + 1]
        yt_ref[qi, hd] = (o[:D_V] * (1.0 / l)).astype(BF16)

    def step(qi, m0):
        m1 = scores(qi, 1)
        values(qi, 0, m0)
        m0_next = scores(qi + 1, 0)
        values(qi, 1, m1)
        return m0_next

    m0 = jax.lax.fori_loop(0, nq - 1, step, scores(0, 0))
    m1 = scores(nq - 1, 1)
    values(nq - 1, 0, m0)
    values(nq - 1, 1, m1)


def _attention(qt, k, vt, batch, seq):
    tq = Q_TILE
    nq = seq // tq
    n_groups = N_HEADS // HEADS_PER_STEP
    return pl.pallas_call(
        _attention_kernel,
        grid=(batch, n_groups),
        in_specs=[pl.BlockSpec((nq, HEADS_PER_STEP, HEAD_PAD, tq), lambda b, g: (b, g, 0, 0)),
                  pl.BlockSpec((HEADS_PER_STEP, seq, HEAD_PAD), lambda b, g: (g, b, 0)),
                  pl.BlockSpec((HEADS_PER_STEP, V_ROWS, seq), lambda b, g: (g, 0, b))],
        out_specs=pl.BlockSpec((nq, HEADS_PER_STEP, D_V, tq), lambda b, g: (b, g, 0, 0)),
        out_shape=jax.ShapeDtypeStruct((batch * nq, N_HEADS, D_V, tq), BF16),
        scratch_shapes=[pltpu.VMEM((seq, tq), F32)] * HEADS_PER_STEP,
        compiler_params=_params(2),
        name="attention",
    )(qt, k, vt)


def _conv3(buf_ref, w, n_rows):
    u = buf_ref[...]
    total = u.shape[0]
    full = (pltpu.roll(w[0:1] * u, 1, axis=0)
            + w[1:2] * u
            + pltpu.roll(w[2:3] * u, total - 1, axis=0))
    return full[HALO:HALO + n_rows]


def _edge_masks(tiles_per_seq):
    j = pl.program_id(0) % tiles_per_seq
    return (j != 0).astype(F32), (j != tiles_per_seq - 1).astype(F32)


def _merge_kernel(tiles_per_seq, x_ref, yt_ref, cb_ref, ccx_ref, ccx_prev_ref, ccx_next_ref,
                  gmix_ref, wgate_ref, bgate_ref, wconv_ref, wau_ref, wcu_ref, wo_ref,
                  out_ref, cs_ref):
    tm = x_ref.shape[0]
    keep_prev, keep_next = _edge_masks(tiles_per_seq)
    x = x_ref[...]
    h = _rms_rows(x, gmix_ref[...]).astype(BF16)

    two = 2 * HALO
    cs_ref[0:HALO, :] = ccx_prev_ref[...].astype(F32)[two - HALO:] * keep_prev
    cs_ref[HALO:HALO + tm, :] = ccx_ref[...].astype(F32)
    cs_ref[HALO + tm:, :] = ccx_next_ref[...].astype(F32)[:HALO] * keep_next
    y_conv = (cb_ref[...].astype(F32) * _conv3(cs_ref, wconv_ref[...], tm)).astype(BF16)

    yt = yt_ref[0].reshape(N_HEADS * D_V, tm)
    y_attn = yt.astype(F32).T.astype(BF16)
    gates = jax.nn.sigmoid(_dot(h, wgate_ref[...]) + bgate_ref[...])
    merged = (gates[:, :D_MODEL] * _dot(y_attn, wau_ref[...])
              + gates[:, D_MODEL:] * _dot(y_conv, wcu_ref[...]))
    out_ref[...] = x + _dot(merged.astype(BF16), wo_ref[...])


def _merge(layer, x2d, yt, cb, ccx, w, seq):
    n_tokens = x2d.shape[0]
    tm = ROW_TILE
    two = 2 * HALO
    per_tile = tm // two
    last_blk = n_tokens // two - 1
    row = lambda width: pl.BlockSpec((tm, width), lambda i: (i, 0))
    gate_block = (N_CONV_IN + LATENT_COLS) // N_GATES
    return pl.pallas_call(
        functools.partial(_merge_kernel, seq // tm),
        grid=(n_tokens // tm,),
        in_specs=[row(D_MODEL),
                  pl.BlockSpec((1, N_HEADS, D_V, tm), lambda i: (i, 0, 0, 0)),
                  row(CONV_WIDTH), row(CONV_WIDTH),
                  pl.BlockSpec((two, CONV_WIDTH), lambda i: (jnp.maximum(i * per_tile - 1, 0), 0)),
                  pl.BlockSpec((two, CONV_WIDTH),
                               lambda i: (jnp.minimum((i + 1) * per_tile, last_blk), 0)),
                  _layer_spec(layer, 1, D_MODEL),
                  _layer_spec(layer, D_MODEL, N_GATES, gate_block),
                  _layer_spec(layer, 1, N_GATES),
                  _layer_spec(layer, 3, CONV_WIDTH),
                  _layer_spec(layer, N_HEADS * D_V, D_MODEL),
                  _layer_spec(layer, CONV_WIDTH, D_MODEL),
                  _layer_spec(layer, D_MODEL, D_MODEL)],
        out_specs=row(D_MODEL),
        out_shape=jax.ShapeDtypeStruct((n_tokens, D_MODEL), F32),
        scratch_shapes=[pltpu.VMEM((tm + 2 * HALO, CONV_WIDTH), F32)],
        compiler_params=_params(1),
        name="merge",
    )(x2d, yt, cb, ccx, ccx, ccx, w["g_mix"], w["w_in"], w["b_gate"], w["w_conv"],
      w["w_attn_up"], w["w_conv_up"], w["w_o"])


def _ffn_kernel(tiles_per_seq, x_ref, xprev_ref, xnext_ref, p_ref, gffn_ref, wup_ref, wfc_ref,
                wdn_ref, gple_ref, wpg_ref, wple_ref, out_ref, lhs_ref, ua0_ref, uv0_ref,
                ua1_ref, uv1_ref, act_ref):
    tm = x_ref.shape[0]
    keep_prev, keep_next = _edge_masks(tiles_per_seq)
    x = x_ref[...]
    gffn = gffn_ref[...]
    lhs_ref[...] = jnp.concatenate(
        [_rms_rows(xprev_ref[...], gffn) * keep_prev,
         _rms_rows(x, gffn),
         _rms_rows(xnext_ref[...], gffn) * keep_next], axis=0).astype(BF16)
    u_bufs = ((ua0_ref, uv0_ref), (ua1_ref, uv1_ref))

    def cols(j, half):
        return pl.ds(half * D_FF + j * FF_CHUNK, FF_CHUNK)

    def up(j):
        lhs = lhs_ref[...]
        ua_ref, uv_ref = u_bufs[j % 2]
        ua_ref[...] = _dot(lhs, wup_ref[:, cols(j, 0)])
        uv_ref[...] = _dot(lhs, wup_ref[:, cols(j, 1)])

    def gate(j):
        ua_ref, uv_ref = u_bufs[j % 2]
        a = _conv3(ua_ref, wfc_ref[:, cols(j, 0)], tm)
        v = _conv3(uv_ref, wfc_ref[:, cols(j, 1)], tm)
        act_ref[:, cols(j, 0)] = (a * jax.nn.sigmoid(a) * v).astype(BF16)

    def down(first, last):
        span = pl.ds(first * FF_CHUNK, (last - first) * FF_CHUNK)
        return _dot(act_ref[:, span], wdn_ref[span, :])

    y = None
    up(0)
    for j in range(N_FF_CHUNKS):
        if j + 1 < N_FF_CHUNKS:
            up(j + 1)
        gate(j)
        if j + 1 in DOWN_SPLITS:
            first = ([0] + list(DOWN_SPLITS))[DOWN_SPLITS.index(j + 1)]
            part = down(first, j + 1)
            y = part if y is None else y + part

    x2 = x + y
    h3 = _rms_rows(x2, gple_ref[...]).astype(BF16)
    pg = jax.nn.sigmoid(_dot(h3, wpg_ref[...]))
    out_ref[...] = x2 + pg * _dot(p_ref[...].astype(BF16), wple_ref[...])


def _ffn(layer, x2d, p3d, w, seq):
    n_tokens = x2d.shape[0]
    tm = ROW_TILE
    per_tile = tm // HALO
    last_blk = n_tokens // HALO - 1
    row = lambda width: pl.BlockSpec((tm, width), lambda i: (i, 0))
    return pl.pallas_call(
        functools.partial(_ffn_kernel, seq // tm),
        grid=(n_tokens // tm,),
        in_specs=[row(D_MODEL),
                  pl.BlockSpec((HALO, D_MODEL), lambda i: (jnp.maximum(i * per_tile - 1, 0), 0)),
                  pl.BlockSpec((HALO, D_MODEL),
                               lambda i: (jnp.minimum((i + 1) * per_tile, last_blk), 0)),
                  pl.BlockSpec((None, tm, PLE_DIM), lambda i: (layer, i, 0)),
                  _layer_spec(layer, 1, D_MODEL),
                  _layer_spec(layer, D_MODEL, 2 * D_FF),
                  _layer_spec(layer, 3, 2 * D_FF),
                  _layer_spec(layer, D_FF, D_MODEL),
                  _layer_spec(layer, 1, D_MODEL),
                  _layer_spec(layer, D_MODEL, D_MODEL),
                  _layer_spec(layer, PLE_DIM, D_MODEL)],
        out_specs=row(D_MODEL),
        out_shape=jax.ShapeDtypeStruct((n_tokens, D_MODEL), F32),
        scratch_shapes=[pltpu.VMEM((tm + 2 * HALO, D_MODEL), BF16)]
        + [pltpu.VMEM((tm + 2 * HALO, FF_CHUNK), F32)] * 4
        + [pltpu.VMEM((tm, D_FF), BF16)],
        compiler_params=_params(1),
        name="ffn",
    )(x2d, x2d, x2d, p3d, w["g_ffn"], w["w_up"], w["w_ffn_conv"], w["w_down"], w["g_ple"],
      w["w_ple_gate"], w["w_ple"])


def _prep_weights(w_in, b_gate, g_mix, g_q_lat, w_uq, g_kv_lat, w_ukv, g_q_head, g_k_head,
                  w_attn_up, w_conv, w_conv_up, w_o, g_ffn, w_up, w_ffn_conv, w_down, g_ple,
                  w_ple_gate, w_ple):
    depth = w_in.shape[0]
    as_row = lambda g: g.reshape(depth, 1, g.shape[-1])
    as_col = lambda g: g.reshape(depth, g.shape[-1], 1)

    lat_pad = jnp.zeros((depth, D_MODEL, LATENT_COLS - N_LATENT), w_in.dtype)
    w_in_b = jnp.concatenate(
        [w_in[:, :, N_LATENT:N_LATENT + N_CONV_IN], w_in[:, :, :N_LATENT], lat_pad,
         w_in[:, :, N_LATENT + N_CONV_IN:]], axis=2).astype(BF16)

    uq = w_uq.reshape(depth, Q_LORA, N_HEADS, D_QK)
    uq = jnp.pad(uq, ((0, 0), (0, 0), (0, 0), (0, HEAD_PAD - D_QK)))
    w_uq_t = uq.reshape(depth, Q_LORA, N_HEADS * HEAD_PAD).transpose(0, 2, 1).astype(BF16)

    ukv = w_ukv.reshape(depth, KV_LORA, N_HEADS, D_NOPE + D_V)
    uk = ukv[..., :D_NOPE].reshape(depth, KV_LORA, N_HEADS * D_NOPE)
    uv = ukv[..., D_NOPE:].reshape(depth, KV_LORA, N_HEADS * D_V)
    w_ukv_t = jnp.concatenate([uk, uv], axis=2).transpose(0, 2, 1).astype(BF16)

    return {
        "g_mix": as_row(g_mix),
        "w_in": w_in_b,
        "b_gate": as_row(b_gate),
        "g_q_lat": as_col(g_q_lat),
        "w_uq_t": w_uq_t,
        "g_kv_lat": as_col(g_kv_lat),
        "w_ukv_t": w_ukv_t,
        "g_q_head": as_col(g_q_head * (D_QK ** -0.5 * LOG2_E)),
        "g_k_head": as_col(g_k_head),
        "w_attn_up": w_attn_up.astype(BF16),
        "w_conv": w_conv,
        "w_conv_up": w_conv_up.astype(BF16),
        "w_o": w_o.astype(BF16),
        "g_ffn": as_row(g_ffn),
        "w_up": w_up.astype(BF16),
        "w_ffn_conv": w_ffn_conv,
        "w_down": w_down.astype(BF16),
        "g_ple": as_row(g_ple),
        "w_ple_gate": w_ple_gate.astype(BF16),
        "w_ple": w_ple.astype(BF16),
    }


def kernel(x, p, positions, w_in, b_gate, g_mix, g_q_lat, w_uq, g_kv_lat, w_ukv, g_q_head,
           g_k_head, w_attn_up, w_conv, w_conv_up, w_o, g_ffn, w_up, w_ffn_conv, w_down,
           g_ple, w_ple_gate, w_ple):
    batch, seq, d_model = x.shape
    depth = w_in.shape[0]
    assert d_model == D_MODEL and seq % ROW_TILE == 0
    n_tokens = batch * seq
    tabs = _rope_tables(positions, n_tokens, seq)
    w = _prep_weights(w_in, b_gate, g_mix, g_q_lat, w_uq, g_kv_lat, w_ukv, g_q_head, g_k_head,
                      w_attn_up, w_conv, w_conv_up, w_o, g_ffn, w_up, w_ffn_conv, w_down,
                      g_ple, w_ple_gate, w_ple)
    x2d = x.reshape(n_tokens, D_MODEL)
    p3d = p.reshape(depth, n_tokens, PLE_DIM)
    for layer in range(depth):
        qt, k, vt, cb, ccx = _inproj(layer, x2d, tabs, w)
        yt = _attention(qt, k, vt, batch, seq)
        x2d = _merge(layer, x2d, yt, cb, ccx, w, seq)
        x2d = _ffn(layer, x2d, p3d, w, seq)
    return x2d.reshape(batch, seq, D_MODEL)
```

```python
import functools

import numpy as np
import jax
import jax.numpy as jnp
from jax.experimental import pallas as pl
from jax.experimental.pallas import tpu as pltpu

D_MODEL = 1024
N_HEADS = 8
D_NOPE = 64
D_ROPE = 32
D_QK = D_NOPE + D_ROPE
D_V = 64
Q_LORA = 256
KV_LORA = 128
CONV_WIDTH = 512
D_FF = 2816
PLE_DIM = 256
EPS = 1e-6
ROPE_THETA = 10000.0

HEAD_PAD = 128
HALF_ROPE = D_ROPE // 2
V_ROWS = D_V + 16
N_LATENT = Q_LORA + KV_LORA + D_ROPE
LATENT_COLS = 512
N_CONV_IN = 3 * CONV_WIDTH
N_GATES = 2 * D_MODEL
HALO = 8
FF_CHUNK = 256
N_FF_CHUNKS = D_FF // FF_CHUNK
DOWN_SPLITS = (N_FF_CHUNKS,)
VMEM_LIMIT_BYTES = 56 * 1024 * 1024

ROW_TILE = 512
Q_TILE = ROW_TILE
KEY_CHUNK = 256
HEADS_PER_STEP = 8
LOG2_E = 1.4426950408889634

BF16 = jnp.bfloat16
F32 = jnp.float32


def _dot(a, b):
    return jnp.dot(a, b, preferred_element_type=F32)


def _rms_rows(v, gain):
    return v * jax.lax.rsqrt(jnp.mean(v * v, axis=-1, keepdims=True) + EPS) * gain


def _rms_cols(v, gain):
    return v * jax.lax.rsqrt(jnp.mean(v * v, axis=0, keepdims=True) + EPS) * gain


def _rope_cols(t, cos, sin):
    t1, t2 = t[:HALF_ROPE], t[HALF_ROPE:]
    return t1 * cos - t2 * sin, t2 * cos + t1 * sin


def _layer_spec(layer, rows, cols, col_block=0):
    return pl.BlockSpec((None, rows, cols), lambda *_: (layer, 0, col_block),
                        pipeline_mode=pl.Buffered(1))


def _params(n_axes, flags=None):
    return pltpu.CompilerParams(
        dimension_semantics=("arbitrary",) * n_axes,
        vmem_limit_bytes=VMEM_LIMIT_BYTES,
        flags=flags)


def _rope_table_kernel(pos_ref, inv_ref, cos_ref, sin_ref):
    ang = pos_ref[...].astype(F32) * inv_ref[...]
    cos_ref[...] = jnp.cos(ang)
    sin_ref[...] = jnp.sin(ang)


def _rope_tables(positions, n_tokens, seq):
    inv = ROPE_THETA ** (-np.arange(0, D_ROPE, 2, dtype=np.float32) / D_ROPE)
    inv = jnp.asarray(inv.astype(np.float32)).reshape(HALF_ROPE, 1)
    pos = positions.reshape(1, n_tokens)
    out = jax.ShapeDtypeStruct((HALF_ROPE, n_tokens), F32)
    return pl.pallas_call(
        _rope_table_kernel,
        grid=(n_tokens // seq,),
        in_specs=[pl.BlockSpec((1, seq), lambda i: (0, i)),
                  pl.BlockSpec((HALF_ROPE, 1), lambda i: (0, 0))],
        out_specs=[pl.BlockSpec((HALF_ROPE, seq), lambda i: (0, i))] * 2,
        out_shape=[out, out],
        compiler_params=_params(1),
        name="rope_tables",
    )(pos, inv)


def _inproj_kernel(x_ref, gmix_ref, wconv_in_ref, wlat_ref, gq_ref, wuq_ref, gkv_ref,
                   wukv_ref, gqh_ref, gkh_ref, cos_ref, sin_ref,
                   qt_ref, k_ref, vt_ref, cb_ref, ccx_ref):
    h = _rms_rows(x_ref[...], gmix_ref[...]).astype(BF16)

    zt = _dot(h, wlat_ref[...]).T

    zc = _dot(h, wconv_in_ref[...])
    cb_ref[...] = zc[:, :CONV_WIDTH].astype(BF16)
    ccx_ref[...] = (zc[:, CONV_WIDTH:2 * CONV_WIDTH] * zc[:, 2 * CONV_WIDTH:]).astype(BF16)

    cos = cos_ref[...]
    sin = sin_ref[...]

    cqn = _rms_cols(zt[:Q_LORA], gq_ref[...]).astype(BF16)
    qt = _dot(wuq_ref[...], cqn)
    gqh = gqh_ref[...]
    for hd in range(N_HEADS):
        blk = qt[hd * HEAD_PAD:(hd + 1) * HEAD_PAD]
        qn = _rms_cols(blk[:D_NOPE], gqh[:D_NOPE])
        qr = _rms_cols(blk[D_NOPE:D_QK], gqh[D_NOPE:])
        r1, r2 = _rope_cols(qr, cos, sin)
        full = jnp.concatenate([qn, r1, r2, blk[D_QK:]], axis=0)
        qt_ref[0, hd] = full.astype(BF16)

    ckvn = _rms_cols(zt[Q_LORA:Q_LORA + KV_LORA], gkv_ref[...]).astype(BF16)
    kvt = _dot(wukv_ref[...], ckvn)
    for hd in range(N_HEADS):
        lo = N_HEADS * D_NOPE + hd * D_V
        vt_ref[hd, 0:D_V, :] = kvt[lo:lo + D_V].astype(BF16)
        vt_ref[hd, D_V:, :] = jnp.ones((V_ROWS - D_V, kvt.shape[1]), BF16)

    gkh = gkh_ref[...]
    kr = _rms_cols(zt[Q_LORA + KV_LORA:N_LATENT], gkh[D_NOPE:])
    kr1, kr2 = _rope_cols(kr, cos, sin)
    pad = jnp.zeros((HEAD_PAD - D_QK, kr.shape[1]), F32)
    for hd in range(N_HEADS):
        kn = _rms_cols(kvt[hd * D_NOPE:(hd + 1) * D_NOPE], gkh[:D_NOPE])
        full = jnp.concatenate([kn, kr1, kr2, pad], axis=0)
        k_ref[hd] = full.T.astype(BF16)


def _inproj(layer, x2d, tabs, w):
    n_tokens = x2d.shape[0]
    tm = ROW_TILE
    cos_t, sin_t = tabs
    row = lambda width: pl.BlockSpec((tm, width), lambda i: (i, 0))
    col = lambda height: pl.BlockSpec((height, tm), lambda i: (0, i))
    outs = [
        jax.ShapeDtypeStruct((n_tokens // tm, N_HEADS, HEAD_PAD, tm), BF16),
        jax.ShapeDtypeStruct((N_HEADS, n_tokens, HEAD_PAD), BF16),
        jax.ShapeDtypeStruct((N_HEADS, V_ROWS, n_tokens), BF16),
        jax.ShapeDtypeStruct((n_tokens, CONV_WIDTH), BF16),
        jax.ShapeDtypeStruct((n_tokens, CONV_WIDTH), BF16),
    ]
    lat_block = N_CONV_IN // LATENT_COLS
    return pl.pallas_call(
        _inproj_kernel,
        grid=(n_tokens // tm,),
        in_specs=[row(D_MODEL),
                  _layer_spec(layer, 1, D_MODEL),
                  _layer_spec(layer, D_MODEL, N_CONV_IN),
                  _layer_spec(layer, D_MODEL, LATENT_COLS, lat_block),
                  _layer_spec(layer, Q_LORA, 1),
                  _layer_spec(layer, N_HEADS * HEAD_PAD, Q_LORA),
                  _layer_spec(layer, KV_LORA, 1),
                  _layer_spec(layer, N_HEADS * (D_NOPE + D_V), KV_LORA),
                  _layer_spec(layer, D_QK, 1),
                  _layer_spec(layer, D_QK, 1),
                  col(HALF_ROPE), col(HALF_ROPE)],
        out_specs=[pl.BlockSpec((1, N_HEADS, HEAD_PAD, tm), lambda i: (i, 0, 0, 0)),
                   pl.BlockSpec((N_HEADS, tm, HEAD_PAD), lambda i: (0, i, 0)),
                   pl.BlockSpec((N_HEADS, V_ROWS, tm), lambda i: (0, 0, i)),
                   row(CONV_WIDTH), row(CONV_WIDTH)],
        out_shape=outs,
        compiler_params=_params(1),
        name="inproj",
    )(x2d, w["g_mix"], w["w_in"], w["w_in"], w["g_q_lat"], w["w_uq_t"], w["g_kv_lat"],
      w["w_ukv_t"], w["g_q_head"], w["g_k_head"], cos_t, sin_t)


def _attention_kernel(qt_ref, k_ref, vt_ref, yt_ref, s0_ref, s1_ref):
    nq, heads = qt_ref.shape[:2]
    s_refs = (s0_ref, s1_ref)

    seq = k_ref.shape[1]
    n_chunks = seq // KEY_CHUNK
    rows = lambda c: pl.ds(c * KEY_CHUNK, KEY_CHUNK)

    def score_chunk(q, hd, c, m):
        s = _dot(k_ref[hd, rows(c), :], q)
        s_refs[hd % 2][rows(c), :] = s
        part = jnp.max(s, axis=0, keepdims=True)
        return part if m is None else jnp.maximum(m, part)

    def value_chunk(hd, c, m, o):
        p = jnp.exp2(s_refs[hd % 2][rows(c), :] - m).astype(BF16)
        part = _dot(vt_ref[hd, :, rows(c)], p)
        return part if o is None else o + part

    def emit(qi, hd, o):
        l = o[D_V:D_V + 1]
        yt_ref[qi, hd] = (o[:D_V] * (1.0 / l)).astype(BF16)

    def scores(qi, hd):
        q = qt_ref[qi, hd]
        m = None
        for c in range(n_chunks):
            m = score_chunk(q, hd, c, m)
        return m

    def overlapped(qi_scores, hd_scores, qi_values, hd_values, m_values):
        q = qt_ref[qi_scores, hd_scores]
        m, o = None, None
        for c in range(n_chunks):
            m = score_chunk(q, hd_scores, c, m)
            o = value_chunk(hd_values, c, m_values, o)
        emit(qi_values, hd_values, o)
        return m

    def step(qi, m):
        for hd in range(heads - 1):
            m = overlapped(qi, hd + 1, qi, hd, m)
        return overlapped(qi + 1, 0, qi, heads - 1, m)

    m = jax.lax.fori_loop(0, nq - 1, step, scores(0, 0))
    for hd in range(heads - 1):
        m = overlapped(nq - 1, hd + 1, nq - 1, hd, m)
    o = None
    for c in range(n_chunks):
        o = value_chunk(heads - 1, c, m, o)
    emit(nq - 1, heads - 1, o)


def _attention(qt, k, vt, batch, seq):
    tq = Q_TILE
    nq = seq // tq
    n_groups = N_HEADS // HEADS_PER_STEP
    return pl.pallas_call(
        _attention_kernel,
        grid=(batch, n_groups),
        in_specs=[pl.BlockSpec((nq, HEADS_PER_STEP, HEAD_PAD, tq), lambda b, g: (b, g, 0, 0)),
                  pl.BlockSpec((HEADS_PER_STEP, seq, HEAD_PAD), lambda b, g: (g, b, 0)),
                  pl.BlockSpec((HEADS_PER_STEP, V_ROWS, seq), lambda b, g: (g, 0, b))],
        out_specs=pl.BlockSpec((nq, HEADS_PER_STEP, D_V, tq), lambda b, g: (b, g, 0, 0)),
        out_shape=jax.ShapeDtypeStruct((batch * nq, N_HEADS, D_V, tq), BF16),
        scratch_shapes=[pltpu.VMEM((seq, tq), F32)] * 2,
        compiler_params=_params(2),
        name="attention",
    )(qt, k, vt)


def _conv3(buf_ref, w, n_rows):
    u = buf_ref[...]
    total = u.shape[0]
    full = (pltpu.roll(w[0:1] * u, 1, axis=0)
            + w[1:2] * u
            + pltpu.roll(w[2:3] * u, total - 1, axis=0))
    return full[HALO:HALO + n_rows]


def _edge_masks(tiles_per_seq):
    j = pl.program_id(0) % tiles_per_seq
    return (j != 0).astype(F32), (j != tiles_per_seq - 1).astype(F32)


def _merge_kernel(tiles_per_seq, x_ref, yt_ref, cb_ref, ccx_ref, ccx_prev_ref, ccx_next_ref,
                  gmix_ref, wgate_ref, bgate_ref, wconv_ref, wau_ref, wcu_ref, wo_ref,
                  out_ref, cs_ref):
    tm = x_ref.shape[0]
    keep_prev, keep_next = _edge_masks(tiles_per_seq)
    x = x_ref[...]
    h = _rms_rows(x, gmix_ref[...]).astype(BF16)

    two = 2 * HALO
    cs_ref[0:HALO, :] = ccx_prev_ref[...].astype(F32)[two - HALO:] * keep_prev
    cs_ref[HALO:HALO + tm, :] = ccx_ref[...].astype(F32)
    cs_ref[HALO + tm:, :] = ccx_next_ref[...].astype(F32)[:HALO] * keep_next
    y_conv = (cb_ref[...].astype(F32) * _conv3(cs_ref, wconv_ref[...], tm)).astype(BF16)

    yt = yt_ref[0].reshape(N_HEADS * D_V, tm)
    y_attn = yt.astype(F32).T.astype(BF16)
    gates = jax.nn.sigmoid(_dot(h, wgate_ref[...]) + bgate_ref[...])
    merged = (gates[:, :D_MODEL] * _dot(y_attn, wau_ref[...])
              + gates[:, D_MODEL:] * _dot(y_conv, wcu_ref[...]))
    out_ref[...] = x + _dot(merged.astype(BF16), wo_ref[...])


def _merge(layer, x2d, yt, cb, ccx, w, seq):
    n_tokens = x2d.shape[0]
    tm = ROW_TILE
    two = 2 * HALO
    per_tile = tm // two
    last_blk = n_tokens // two - 1
    row = lambda width: pl.BlockSpec((tm, width), lambda i: (i, 0))
    gate_block = (N_CONV_IN + LATENT_COLS) // N_GATES
    return pl.pallas_call(
        functools.partial(_merge_kernel, seq // tm),
        grid=(n_tokens // tm,),
        in_specs=[row(D_MODEL),
                  pl.BlockSpec((1, N_HEADS, D_V, tm), lambda i: (i, 0, 0, 0)),
                  row(CONV_WIDTH), row(CONV_WIDTH),
                  pl.BlockSpec((two, CONV_WIDTH), lambda i: (jnp.maximum(i * per_tile - 1, 0), 0)),
                  pl.BlockSpec((two, CONV_WIDTH),
                               lambda i: (jnp.minimum((i + 1) * per_tile, last_blk), 0)),
                  _layer_spec(layer, 1, D_MODEL),
                  _layer_spec(layer, D_MODEL, N_GATES, gate_block),
                  _layer_spec(layer, 1, N_GATES),
                  _layer_spec(layer, 3, CONV_WIDTH),
                  _layer_spec(layer, N_HEADS * D_V, D_MODEL),
                  _layer_spec(layer, CONV_WIDTH, D_MODEL),
                  _layer_spec(layer, D_MODEL, D_MODEL)],
        out_specs=row(D_MODEL),
        out_shape=jax.ShapeDtypeStruct((n_tokens, D_MODEL), F32),
        scratch_shapes=[pltpu.VMEM((tm + 2 * HALO, CONV_WIDTH), F32)],
        compiler_params=_params(1),
        name="merge",
    )(x2d, yt, cb, ccx, ccx, ccx, w["g_mix"], w["w_in"], w["b_gate"], w["w_conv"],
      w["w_attn_up"], w["w_conv_up"], w["w_o"])


def _ffn_kernel(tiles_per_seq, x_ref, xprev_ref, xnext_ref, p_ref, gffn_ref, wup_ref, wfc_ref,
                wdn_ref, gple_ref, wpg_ref, wple_ref, out_ref, lhs_ref, ua0_ref, uv0_ref,
                ua1_ref, uv1_ref, act_ref):
    tm = x_ref.shape[0]
    keep_prev, keep_next = _edge_masks(tiles_per_seq)
    x = x_ref[...]
    gffn = gffn_ref[...]
    lhs_ref[...] = jnp.concatenate(
        [_rms_rows(xprev_ref[...], gffn) * keep_prev,
         _rms_rows(x, gffn),
         _rms_rows(xnext_ref[...], gffn) * keep_next], axis=0).astype(BF16)
    u_bufs = ((ua0_ref, uv0_ref), (ua1_ref, uv1_ref))

    def cols(j, half):
        return pl.ds(half * D_FF + j * FF_CHUNK, FF_CHUNK)

    def up(j):
        lhs = lhs_ref[...]
        ua_ref, uv_ref = u_bufs[j % 2]
        ua_ref[...] = _dot(lhs, wup_ref[:, cols(j, 0)])
        uv_ref[...] = _dot(lhs, wup_ref[:, cols(j, 1)])

    def gate(j):
        ua_ref, uv_ref = u_bufs[j % 2]
        a = _conv3(ua_ref, wfc_ref[:, cols(j, 0)], tm)
        v = _conv3(uv_ref, wfc_ref[:, cols(j, 1)], tm)
        act_ref[:, cols(j, 0)] = (a * jax.nn.sigmoid(a) * v).astype(BF16)

    def down(first, last):
        span = pl.ds(first * FF_CHUNK, (last - first) * FF_CHUNK)
        return _dot(act_ref[:, span], wdn_ref[span, :])

    y = None
    up(0)
    for j in range(N_FF_CHUNKS):
        if j + 1 < N_FF_CHUNKS:
            up(j + 1)
        gate(j)
        if j + 1 in DOWN_SPLITS:
            first = ([0] + list(DOWN_SPLITS))[DOWN_SPLITS.index(j + 1)]
            part = down(first, j + 1)
            y = part if y is None else y + part

    x2 = x + y
    h3 = _rms_rows(x2, gple_ref[...]).astype(BF16)
    pg = jax.nn.sigmoid(_dot(h3, wpg_ref[...]))
    out_ref[...] = x2 + pg * _dot(p_ref[...].astype(BF16), wple_ref[...])


def _ffn(layer, x2d, p3d, w, seq):
    n_tokens = x2d.shape[0]
    tm = ROW_TILE
    per_tile = tm // HALO
    last_blk = n_tokens // HALO - 1
    row = lambda width: pl.BlockSpec((tm, width), lambda i: (i, 0))
    return pl.pallas_call(
        functools.partial(_ffn_kernel, seq // tm),
        grid=(n_tokens // tm,),
        in_specs=[row(D_MODEL),
                  pl.BlockSpec((HALO, D_MODEL), lambda i: (jnp.maximum(i * per_tile - 1, 0), 0)),
                  pl.BlockSpec((HALO, D_MODEL),
                               lambda i: (jnp.minimum((i + 1) * per_tile, last_blk), 0)),
                  pl.BlockSpec((None, tm, PLE_DIM), lambda i: (layer, i, 0)),
                  _layer_spec(layer, 1, D_MODEL),
                  _layer_spec(layer, D_MODEL, 2 * D_FF),
                  _layer_spec(layer, 3, 2 * D_FF),
                  _layer_spec(layer, D_FF, D_MODEL),
                  _layer_spec(layer, 1, D_MODEL),
                  _layer_spec(layer, D_MODEL, D_MODEL),
                  _layer_spec(layer, PLE_DIM, D_MODEL)],
        out_specs=row(D_MODEL),
        out_shape=jax.ShapeDtypeStruct((n_tokens, D_MODEL), F32),
        scratch_shapes=[pltpu.VMEM((tm + 2 * HALO, D_MODEL), BF16)]
        + [pltpu.VMEM((tm + 2 * HALO, FF_CHUNK), F32)] * 4
        + [pltpu.VMEM((tm, D_FF), BF16)],
        compiler_params=_params(1),
        name="ffn",
    )(x2d, x2d, x2d, p3d, w["g_ffn"], w["w_up"], w["w_ffn_conv"], w["w_down"], w["g_ple"],
      w["w_ple_gate"], w["w_ple"])


def _prep_weights(w_in, b_gate, g_mix, g_q_lat, w_uq, g_kv_lat, w_ukv, g_q_head, g_k_head,
                  w_attn_up, w_conv, w_conv_up, w_o, g_ffn, w_up, w_ffn_conv, w_down, g_ple,
                  w_ple_gate, w_ple):
    depth = w_in.shape[0]
    as_row = lambda g: g.reshape(depth, 1, g.shape[-1])
    as_col = lambda g: g.reshape(depth, g.shape[-1], 1)

    lat_pad = jnp.zeros((depth, D_MODEL, LATENT_COLS - N_LATENT), w_in.dtype)
    w_in_b = jnp.concatenate(
        [w_in[:, :, N_LATENT:N_LATENT + N_CONV_IN], w_in[:, :, :N_LATENT], lat_pad,
         w_in[:, :, N_LATENT + N_CONV_IN:]], axis=2).astype(BF16)

    uq = w_uq.reshape(depth, Q_LORA, N_HEADS, D_QK)
    uq = jnp.pad(uq, ((0, 0), (0, 0), (0, 0), (0, HEAD_PAD - D_QK)))
    w_uq_t = uq.reshape(depth, Q_LORA, N_HEADS * HEAD_PAD).transpose(0, 2, 1).astype(BF16)

    ukv = w_ukv.reshape(depth, KV_LORA, N_HEADS, D_NOPE + D_V)
    uk = ukv[..., :D_NOPE].reshape(depth, KV_LORA, N_HEADS * D_NOPE)
    uv = ukv[..., D_NOPE:].reshape(depth, KV_LORA, N_HEADS * D_V)
    w_ukv_t = jnp.concatenate([uk, uv], axis=2).transpose(0, 2, 1).astype(BF16)

    return {
        "g_mix": as_row(g_mix),
        "w_in": w_in_b,
        "b_gate": as_row(b_gate),
        "g_q_lat": as_col(g_q_lat),
        "w_uq_t": w_uq_t,
        "g_kv_lat": as_col(g_kv_lat),
        "w_ukv_t": w_ukv_t,
        "g_q_head": as_col(g_q_head * (D_QK ** -0.5 * LOG2_E)),
        "g_k_head": as_col(g_k_head),
        "w_attn_up": w_attn_up.astype(BF16),
        "w_conv": w_conv,
        "w_conv_up": w_conv_up.astype(BF16),
        "w_o": w_o.astype(BF16),
        "g_ffn": as_row(g_ffn),
        "w_up": w_up.astype(BF16),
        "w_ffn_conv": w_ffn_conv,
        "w_down": w_down.astype(BF16),
        "g_ple": as_row(g_ple),
        "w_ple_gate": w_ple_gate.astype(BF16),
        "w_ple": w_ple.astype(BF16),
    }


def kernel(x, p, positions, w_in, b_gate, g_mix, g_q_lat, w_uq, g_kv_lat, w_ukv, g_q_head,
           g_k_head, w_attn_up, w_conv, w_conv_up, w_o, g_ffn, w_up, w_ffn_conv, w_down,
           g_ple, w_ple_gate, w_ple):
    batch, seq, d_model = x.shape
    depth = w_in.shape[0]
    assert d_model == D_MODEL and seq % ROW_TILE == 0
    n_tokens = batch * seq
    tabs = _rope_tables(positions, n_tokens, seq)
    w = _prep_weights(w_in, b_gate, g_mix, g_q_lat, w_uq, g_kv_lat, w_ukv, g_q_head, g_k_head,
                      w_attn_up, w_conv, w_conv_up, w_o, g_ffn, w_up, w_ffn_conv, w_down,
                      g_ple, w_ple_gate, w_ple)
    x2d = x.reshape(n_tokens, D_MODEL)
    p3d = p.reshape(depth, n_tokens, PLE_DIM)
    for layer in range(depth):
        qt, k, vt, cb, ccx = _inproj(layer, x2d, tabs, w)
        yt = _attention(qt, k, vt, batch, seq)
        x2d = _merge(layer, x2d, yt, cb, ccx, w, seq)
        x2d = _ffn(layer, x2d, p3d, w, seq)
    return x2d.reshape(batch, seq, D_MODEL)
```

```python
import functools

import numpy as np
import jax
import jax.numpy as jnp
from jax.experimental import pallas as pl
from jax.experimental.pallas import tpu as pltpu

D_MODEL = 1024
N_HEADS = 8
D_NOPE = 64
D_ROPE = 32
D_QK = D_NOPE + D_ROPE
D_V = 64
Q_LORA = 256
KV_LORA = 128
CONV_WIDTH = 512
D_FF = 2816
PLE_DIM = 256
EPS = 1e-6
ROPE_THETA = 10000.0

HEAD_PAD = 128
HALF_ROPE = D_ROPE // 2
V_ROWS = D_V + 16
N_LATENT = Q_LORA + KV_LORA + D_ROPE
LATENT_COLS = 512
N_CONV_IN = 3 * CONV_WIDTH
N_GATES = 2 * D_MODEL
W_IN_WINDOW = 2048
assert N_LATENT + N_CONV_IN <= W_IN_WINDOW
WEIGHT_PREP_ROWS = 128
HALO = 8
FF_CHUNK = 256
N_FF_CHUNKS = D_FF // FF_CHUNK
DOWN_SPLITS = (N_FF_CHUNKS,)
VMEM_LIMIT_BYTES = 56 * 1024 * 1024

ROW_TILE = 1024
Q_TILE = 512
Q_PER_ROW_TILE = ROW_TILE // Q_TILE
FFN_TILE = 512
KEY_CHUNK = 256
HEADS_PER_STEP = 8
LOG2_E = 1.4426950408889634

BF16 = jnp.bfloat16
F32 = jnp.float32


def _dot(a, b):
    return jnp.dot(a, b, preferred_element_type=F32)


def _rms_rows(v, gain):
    return v * jax.lax.rsqrt(jnp.mean(v * v, axis=-1, keepdims=True) + EPS) * gain


def _rms_cols(v, gain):
    return v * jax.lax.rsqrt(jnp.mean(v * v, axis=0, keepdims=True) + EPS) * gain


def _rope_cols(t, cos, sin):
    t1, t2 = t[:HALF_ROPE], t[HALF_ROPE:]
    return t1 * cos - t2 * sin, t2 * cos + t1 * sin


def _layer_spec(layer, rows, cols, col_block=0):
    return pl.BlockSpec((None, rows, cols), lambda *_: (layer, 0, col_block),
                        pipeline_mode=pl.Buffered(1))


def _params(n_axes, flags=None):
    return pltpu.CompilerParams(
        dimension_semantics=("arbitrary",) * n_axes,
        vmem_limit_bytes=VMEM_LIMIT_BYTES,
        flags=flags)


def _rope_table_kernel(pos_ref, inv_ref, cos_ref, sin_ref):
    ang = pos_ref[...].astype(F32) * inv_ref[...]
    cos_ref[...] = jnp.cos(ang)
    sin_ref[...] = jnp.sin(ang)


def _rope_tables(positions, n_tokens, seq):
    inv = ROPE_THETA ** (-np.arange(0, D_ROPE, 2, dtype=np.float32) / D_ROPE)
    inv = jnp.asarray(inv.astype(np.float32)).reshape(HALF_ROPE, 1)
    pos = positions.reshape(1, n_tokens)
    out = jax.ShapeDtypeStruct((HALF_ROPE, n_tokens), F32)
    return pl.pallas_call(
        _rope_table_kernel,
        grid=(n_tokens // seq,),
        in_specs=[pl.BlockSpec((1, seq), lambda i: (0, i)),
                  pl.BlockSpec((HALF_ROPE, 1), lambda i: (0, 0))],
        out_specs=[pl.BlockSpec((HALF_ROPE, seq), lambda i: (0, i))] * 2,
        out_shape=[out, out],
        compiler_params=_params(1),
        name="rope_tables",
    )(pos, inv)


def _inproj_kernel(x_ref, gmix_ref, win_ref, gq_ref, wuq_ref, gkv_ref,
                   wukv_ref, gqh_ref, gkh_ref, cos_ref, sin_ref,
                   qt_ref, k_ref, vt_ref, cb_ref, ccx_ref, wlat_ref, wconv_in_ref):
    @pl.when(pl.program_id(0) == 0)
    def _():
        def cast_rows(r, carry):
            rows = pl.ds(pl.multiple_of(r * WEIGHT_PREP_ROWS, WEIGHT_PREP_ROWS), WEIGHT_PREP_ROWS)
            wlat_ref[rows, :] = win_ref[rows, :LATENT_COLS].astype(BF16)
            wconv_in_ref[rows, :] = win_ref[rows, N_LATENT:N_LATENT + N_CONV_IN].astype(BF16)
            return carry
        jax.lax.fori_loop(0, D_MODEL // WEIGHT_PREP_ROWS, cast_rows, 0)

    h = _rms_rows(x_ref[...], gmix_ref[...]).astype(BF16)

    zt = _dot(h, wlat_ref[...]).T

    zc = _dot(h, wconv_in_ref[...])
    cb_ref[...] = zc[:, :CONV_WIDTH].astype(BF16)
    ccx_ref[...] = (zc[:, CONV_WIDTH:2 * CONV_WIDTH] * zc[:, 2 * CONV_WIDTH:]).astype(BF16)

    cos = cos_ref[...]
    sin = sin_ref[...]

    cqn = _rms_cols(zt[:Q_LORA], gq_ref[...]).astype(BF16)
    qt = _dot(wuq_ref[...], cqn)
    gqh = gqh_ref[...]
    for hd in range(N_HEADS):
        blk = qt[hd * HEAD_PAD:(hd + 1) * HEAD_PAD]
        qn = _rms_cols(blk[:D_NOPE], gqh[:D_NOPE])
        qr = _rms_cols(blk[D_NOPE:D_QK], gqh[D_NOPE:])
        r1, r2 = _rope_cols(qr, cos, sin)
        full = jnp.concatenate([qn, r1, r2, blk[D_QK:]], axis=0)
        full = full.astype(BF16)
        for j in range(Q_PER_ROW_TILE):
            qt_ref[j, hd] = full[:, j * Q_TILE:(j + 1) * Q_TILE]

    ckvn = _rms_cols(zt[Q_LORA:Q_LORA + KV_LORA], gkv_ref[...]).astype(BF16)
    kvt = _dot(wukv_ref[...], ckvn)
    for hd in range(N_HEADS):
        lo = N_HEADS * D_NOPE + hd * D_V
        vt_ref[hd, 0:D_V, :] = kvt[lo:lo + D_V].astype(BF16)
        vt_ref[hd, D_V:, :] = jnp.ones((V_ROWS - D_V, kvt.shape[1]), BF16)

    gkh = gkh_ref[...]
    kr = _rms_cols(zt[Q_LORA + KV_LORA:N_LATENT], gkh[D_NOPE:])
    kr1, kr2 = _rope_cols(kr, cos, sin)
    pad = jnp.zeros((HEAD_PAD - D_QK, kr.shape[1]), F32)
    for hd in range(N_HEADS):
        kn = _rms_cols(kvt[hd * D_NOPE:(hd + 1) * D_NOPE], gkh[:D_NOPE])
        full = jnp.concatenate([kn, kr1, kr2, pad], axis=0)
        k_ref[hd] = full.T.astype(BF16)


def _inproj(layer, x2d, tabs, w):
    n_tokens = x2d.shape[0]
    tm = ROW_TILE
    cos_t, sin_t = tabs
    row = lambda width: pl.BlockSpec((tm, width), lambda i: (i, 0))
    col = lambda height: pl.BlockSpec((height, tm), lambda i: (0, i))
    outs = [
        jax.ShapeDtypeStruct((n_tokens // Q_TILE, N_HEADS, HEAD_PAD, Q_TILE), BF16),
        jax.ShapeDtypeStruct((N_HEADS, n_tokens, HEAD_PAD), BF16),
        jax.ShapeDtypeStruct((N_HEADS, V_ROWS, n_tokens), BF16),
        jax.ShapeDtypeStruct((n_tokens, CONV_WIDTH), BF16),
        jax.ShapeDtypeStruct((n_tokens, CONV_WIDTH), BF16),
    ]
    return pl.pallas_call(
        _inproj_kernel,
        grid=(n_tokens // tm,),
        in_specs=[row(D_MODEL),
                  _layer_spec(layer, 1, D_MODEL),
                  _layer_spec(layer, D_MODEL, W_IN_WINDOW),
                  _layer_spec(layer, Q_LORA, 1),
                  _layer_spec(layer, N_HEADS * HEAD_PAD, Q_LORA),
                  _layer_spec(layer, KV_LORA, 1),
                  _layer_spec(layer, N_HEADS * (D_NOPE + D_V), KV_LORA),
                  _layer_spec(layer, D_QK, 1),
                  _layer_spec(layer, D_QK, 1),
                  col(HALF_ROPE), col(HALF_ROPE)],
        out_specs=[pl.BlockSpec((Q_PER_ROW_TILE, N_HEADS, HEAD_PAD, Q_TILE), lambda i: (i, 0, 0, 0)),
                   pl.BlockSpec((N_HEADS, tm, HEAD_PAD), lambda i: (0, i, 0)),
                   pl.BlockSpec((N_HEADS, V_ROWS, tm), lambda i: (0, 0, i)),
                   row(CONV_WIDTH), row(CONV_WIDTH)],
        out_shape=outs,
        scratch_shapes=[pltpu.VMEM((D_MODEL, LATENT_COLS), BF16),
                        pltpu.VMEM((D_MODEL, N_CONV_IN), BF16)],
        compiler_params=_params(1),
        name="inproj",
    )(x2d, w["g_mix"], w["w_in"], w["g_q_lat"], w["w_uq_t"], w["g_kv_lat"],
      w["w_ukv_t"], w["g_q_head"], w["g_k_head"], cos_t, sin_t)


def _attention_kernel(qt_ref, k_ref, vt_ref, yt_ref, s0_ref, s1_ref):
    nq, heads = qt_ref.shape[:2]
    s_refs = (s0_ref, s1_ref)

    seq = k_ref.shape[1]
    n_chunks = seq // KEY_CHUNK
    rows = lambda c: pl.ds(c * KEY_CHUNK, KEY_CHUNK)

    def score_chunk(q, hd, c, m):
        s = _dot(k_ref[hd, rows(c), :], q)
        s_refs[hd % 2][rows(c), :] = s
        part = jnp.max(s, axis=0, keepdims=True)
        return part if m is None else jnp.maximum(m, part)

    def value_chunk(hd, c, m, o):
        p = jnp.exp2(s_refs[hd % 2][rows(c), :] - m).astype(BF16)
        part = _dot(vt_ref[hd, :, rows(c)], p)
        return part if o is None else o + part

    def emit(qi, hd, o):
        l = o[D_V:D_V + 1]
        yt_ref[qi, hd] = (o[:D_V] * (1.0 / l)).astype(BF16)

    def scores(qi, hd):
        q = qt_ref[qi, hd]
        m = None
        for c in range(n_chunks):
            m = score_chunk(q, hd, c, m)
        return m

    def overlapped(qi_scores, hd_scores, qi_values, hd_values, m_values):
        q = qt_ref[qi_scores, hd_scores]
        m, o = None, None
        for c in range(n_chunks):
            m = score_chunk(q, hd_scores, c, m)
            o = value_chunk(hd_values, c, m_values, o)
        emit(qi_values, hd_values, o)
        return m

    def step(qi, m):
        for hd in range(heads - 1):
            m = overlapped(qi, hd + 1, qi, hd, m)
        return overlapped(qi + 1, 0, qi, heads - 1, m)

    m = jax.lax.fori_loop(0, nq - 1, step, scores(0, 0))
    for hd in range(heads - 1):
        m = overlapped(nq - 1, hd + 1, nq - 1, hd, m)
    o = None
    for c in range(n_chunks):
        o = value_chunk(heads - 1, c, m, o)
    emit(nq - 1, heads - 1, o)


def _attention(qt, k, vt, batch, seq):
    tq = Q_TILE
    nq = seq // tq
    n_groups = N_HEADS // HEADS_PER_STEP
    return pl.pallas_call(
        _attention_kernel,
        grid=(batch, n_groups),
        in_specs=[pl.BlockSpec((nq, HEADS_PER_STEP, HEAD_PAD, tq), lambda b, g: (b, g, 0, 0)),
                  pl.BlockSpec((HEADS_PER_STEP, seq, HEAD_PAD), lambda b, g: (g, b, 0)),
                  pl.BlockSpec((HEADS_PER_STEP, V_ROWS, seq), lambda b, g: (g, 0, b))],
        out_specs=pl.BlockSpec((nq, HEADS_PER_STEP, D_V, tq), lambda b, g: (b, g, 0, 0)),
        out_shape=jax.ShapeDtypeStruct((batch * nq, N_HEADS, D_V, tq), BF16),
        scratch_shapes=[pltpu.VMEM((seq, tq), F32)] * 2,
        compiler_params=_params(2),
        name="attention",
    )(qt, k, vt)


def _conv3(buf_ref, w, n_rows):
    u = buf_ref[...]
    total = u.shape[0]
    full = (pltpu.roll(w[0:1] * u, 1, axis=0)
            + w[1:2] * u
            + pltpu.roll(w[2:3] * u, total - 1, axis=0))
    return full[HALO:HALO + n_rows]


def _edge_masks(tiles_per_seq):
    j = pl.program_id(0) % tiles_per_seq
    return (j != 0).astype(F32), (j != tiles_per_seq - 1).astype(F32)


def _merge_kernel(tiles_per_seq, x_ref, yt_ref, cb_ref, ccx_ref, ccx_prev_ref, ccx_next_ref,
                  gmix_ref, wgate_ref, bgate_ref, wconv_ref, wau_ref, wcu_ref, wo_ref,
                  out_ref, cs_ref):
    tm = x_ref.shape[0]
    keep_prev, keep_next = _edge_masks(tiles_per_seq)
    x = x_ref[...]
    h = _rms_rows(x, gmix_ref[...]).astype(BF16)

    two = 2 * HALO
    cs_ref[0:HALO, :] = ccx_prev_ref[...].astype(F32)[two - HALO:] * keep_prev
    cs_ref[HALO:HALO + tm, :] = ccx_ref[...].astype(F32)
    cs_ref[HALO + tm:, :] = ccx_next_ref[...].astype(F32)[:HALO] * keep_next
    y_conv = (cb_ref[...].astype(F32) * _conv3(cs_ref, wconv_ref[...], tm)).astype(BF16)

    y_attn = jnp.concatenate(
        [yt_ref[j].reshape(N_HEADS * D_V, Q_TILE).astype(F32).T for j in range(Q_PER_ROW_TILE)],
        axis=0).astype(BF16)
    gates = jax.nn.sigmoid(_dot(h, wgate_ref[...]) + bgate_ref[...])
    merged = (gates[:, :D_MODEL] * _dot(y_attn, wau_ref[...])
              + gates[:, D_MODEL:] * _dot(y_conv, wcu_ref[...]))
    out_ref[...] = x + _dot(merged.astype(BF16), wo_ref[...])


def _merge(layer, x2d, yt, cb, ccx, w, seq):
    n_tokens = x2d.shape[0]
    tm = ROW_TILE
    two = 2 * HALO
    per_tile = tm // two
    last_blk = n_tokens // two - 1
    row = lambda width: pl.BlockSpec((tm, width), lambda i: (i, 0))
    return pl.pallas_call(
        functools.partial(_merge_kernel, seq // tm),
        grid=(n_tokens // tm,),
        in_specs=[row(D_MODEL),
                  pl.BlockSpec((Q_PER_ROW_TILE, N_HEADS, D_V, Q_TILE), lambda i: (i, 0, 0, 0)),
                  row(CONV_WIDTH), row(CONV_WIDTH),
                  pl.BlockSpec((two, CONV_WIDTH), lambda i: (jnp.maximum(i * per_tile - 1, 0), 0)),
                  pl.BlockSpec((two, CONV_WIDTH),
                               lambda i: (jnp.minimum((i + 1) * per_tile, last_blk), 0)),
                  _layer_spec(layer, 1, D_MODEL),
                  _layer_spec(layer, D_MODEL, N_GATES),
                  _layer_spec(layer, 1, N_GATES),
                  _layer_spec(layer, 3, CONV_WIDTH),
                  _layer_spec(layer, N_HEADS * D_V, D_MODEL),
                  _layer_spec(layer, CONV_WIDTH, D_MODEL),
                  _layer_spec(layer, D_MODEL, D_MODEL)],
        out_specs=row(D_MODEL),
        out_shape=jax.ShapeDtypeStruct((n_tokens, D_MODEL), F32),
        scratch_shapes=[pltpu.VMEM((tm + 2 * HALO, CONV_WIDTH), F32)],
        compiler_params=_params(1),
        name="merge",
    )(x2d, yt, cb, ccx, ccx, ccx, w["g_mix"], w["w_gate"], w["b_gate"], w["w_conv"],
      w["w_attn_up"], w["w_conv_up"], w["w_o"])


def _ffn_kernel(tiles_per_seq, x_ref, xprev_ref, xnext_ref, p_ref, gffn_ref, wup_ref, wfc_ref,
                wdn_ref, gple_ref, wpg_ref, wple_ref, out_ref, lhs_ref, ua0_ref, uv0_ref,
                ua1_ref, uv1_ref, act_ref):
    tm = x_ref.shape[0]
    keep_prev, keep_next = _edge_masks(tiles_per_seq)
    x = x_ref[...]
    gffn = gffn_ref[...]
    lhs_ref[...] = jnp.concatenate(
        [_rms_rows(xprev_ref[...], gffn) * keep_prev,
         _rms_rows(x, gffn),
         _rms_rows(xnext_ref[...], gffn) * keep_next], axis=0).astype(BF16)
    u_bufs = ((ua0_ref, uv0_ref), (ua1_ref, uv1_ref))

    def cols(j, half):
        return pl.ds(half * D_FF + j * FF_CHUNK, FF_CHUNK)

    def up(j):
        lhs = lhs_ref[...]
        ua_ref, uv_ref = u_bufs[j % 2]
        ua_ref[...] = _dot(lhs, wup_ref[:, cols(j, 0)])
        uv_ref[...] = _dot(lhs, wup_ref[:, cols(j, 1)])

    def gate(j):
        ua_ref, uv_ref = u_bufs[j % 2]
        a = _conv3(ua_ref, wfc_ref[:, cols(j, 0)], tm)
        v = _conv3(uv_ref, wfc_ref[:, cols(j, 1)], tm)
        act_ref[:, cols(j, 0)] = (a * jax.nn.sigmoid(a) * v).astype(BF16)

    def down(first, last):
        span = pl.ds(first * FF_CHUNK, (last - first) * FF_CHUNK)
        return _dot(act_ref[:, span], wdn_ref[span, :])

    y = None
    up(0)
    for j in range(N_FF_CHUNKS):
        if j + 1 < N_FF_CHUNKS:
            up(j + 1)
        gate(j)
        if j + 1 in DOWN_SPLITS:
            first = ([0] + list(DOWN_SPLITS))[DOWN_SPLITS.index(j + 1)]
            part = down(first, j + 1)
            y = part if y is None else y + part

    x2 = x + y
    h3 = _rms_rows(x2, gple_ref[...]).astype(BF16)
    pg = jax.nn.sigmoid(_dot(h3, wpg_ref[...]))
    out_ref[...] = x2 + pg * _dot(p_ref[...].astype(BF16), wple_ref[...])


def _ffn(layer, x2d, p3d, w, seq):
    n_tokens = x2d.shape[0]
    tm = FFN_TILE
    per_tile = tm // HALO
    last_blk = n_tokens // HALO - 1
    row = lambda width: pl.BlockSpec((tm, width), lambda i: (i, 0))
    return pl.pallas_call(
        functools.partial(_ffn_kernel, seq // tm),
        grid=(n_tokens // tm,),
        in_specs=[row(D_MODEL),
                  pl.BlockSpec((HALO, D_MODEL), lambda i: (jnp.maximum(i * per_tile - 1, 0), 0)),
                  pl.BlockSpec((HALO, D_MODEL),
                               lambda i: (jnp.minimum((i + 1) * per_tile, last_blk), 0)),
                  pl.BlockSpec((None, tm, PLE_DIM), lambda i: (layer, i, 0)),
                  _layer_spec(layer, 1, D_MODEL),
                  _layer_spec(layer, D_MODEL, 2 * D_FF),
                  _layer_spec(layer, 3, 2 * D_FF),
                  _layer_spec(layer, D_FF, D_MODEL),
                  _layer_spec(layer, 1, D_MODEL),
                  _layer_spec(layer, D_MODEL, D_MODEL),
                  _layer_spec(layer, PLE_DIM, D_MODEL)],
        out_specs=row(D_MODEL),
        out_shape=jax.ShapeDtypeStruct((n_tokens, D_MODEL), F32),
        scratch_shapes=[pltpu.VMEM((tm + 2 * HALO, D_MODEL), BF16)]
        + [pltpu.VMEM((tm + 2 * HALO, FF_CHUNK), F32)] * 4
        + [pltpu.VMEM((tm, D_FF), BF16)],
        compiler_params=_params(1),
        name="ffn",
    )(x2d, x2d, x2d, p3d, w["g_ffn"], w["w_up"], w["w_ffn_conv"], w["w_down"], w["g_ple"],
      w["w_ple_gate"], w["w_ple"])


def _prep_weights(w_in, b_gate, g_mix, g_q_lat, w_uq, g_kv_lat, w_ukv, g_q_head, g_k_head,
                  w_attn_up, w_conv, w_conv_up, w_o, g_ffn, w_up, w_ffn_conv, w_down, g_ple,
                  w_ple_gate, w_ple):
    depth = w_in.shape[0]
    as_row = lambda g: g.reshape(depth, 1, g.shape[-1])
    as_col = lambda g: g.reshape(depth, g.shape[-1], 1)


    uq = w_uq.reshape(depth, Q_LORA, N_HEADS, D_QK)
    uq = jnp.pad(uq, ((0, 0), (0, 0), (0, 0), (0, HEAD_PAD - D_QK)))
    w_uq_t = uq.reshape(depth, Q_LORA, N_HEADS * HEAD_PAD).transpose(0, 2, 1).astype(BF16)

    ukv = w_ukv.reshape(depth, KV_LORA, N_HEADS, D_NOPE + D_V)
    uk = ukv[..., :D_NOPE].reshape(depth, KV_LORA, N_HEADS * D_NOPE)
    uv = ukv[..., D_NOPE:].reshape(depth, KV_LORA, N_HEADS * D_V)
    w_ukv_t = jnp.concatenate([uk, uv], axis=2).transpose(0, 2, 1).astype(BF16)

    return {
        "g_mix": as_row(g_mix),
        "w_in": w_in,
        "w_gate": w_in[:, :, N_LATENT + N_CONV_IN:].astype(BF16),
        "b_gate": as_row(b_gate),
        "g_q_lat": as_col(g_q_lat),
        "w_uq_t": w_uq_t,
        "g_kv_lat": as_col(g_kv_lat),
        "w_ukv_t": w_ukv_t,
        "g_q_head": as_col(g_q_head * (D_QK ** -0.5 * LOG2_E)),
        "g_k_head": as_col(g_k_head),
        "w_attn_up": w_attn_up.astype(BF16),
        "w_conv": w_conv,
        "w_conv_up": w_conv_up.astype(BF16),
        "w_o": w_o.astype(BF16),
        "g_ffn": as_row(g_ffn),
        "w_up": w_up.astype(BF16),
        "w_ffn_conv": w_ffn_conv,
        "w_down": w_down.astype(BF16),
        "g_ple": as_row(g_ple),
        "w_ple_gate": w_ple_gate.astype(BF16),
        "w_ple": w_ple.astype(BF16),
    }


def kernel(x, p, positions, w_in, b_gate, g_mix, g_q_lat, w_uq, g_kv_lat, w_ukv, g_q_head,
           g_k_head, w_attn_up, w_conv, w_conv_up, w_o, g_ffn, w_up, w_ffn_conv, w_down,
           g_ple, w_ple_gate, w_ple):
    batch, seq, d_model = x.shape
    depth = w_in.shape[0]
    assert d_model == D_MODEL and seq % ROW_TILE == 0 and seq % FFN_TILE == 0
    n_tokens = batch * seq
    tabs = _rope_tables(positions, n_tokens, seq)
    w = _prep_weights(w_in, b_gate, g_mix, g_q_lat, w_uq, g_kv_lat, w_ukv, g_q_head, g_k_head,
                      w_attn_up, w_conv, w_conv_up, w_o, g_ffn, w_up, w_ffn_conv, w_down,
                      g_ple, w_ple_gate, w_ple)
    x2d = x.reshape(n_tokens, D_MODEL)
    p3d = p.reshape(depth, n_tokens, PLE_DIM)
    for layer in range(depth):
        qt, k, vt, cb, ccx = _inproj(layer, x2d, tabs, w)
        yt = _attention(qt, k, vt, batch, seq)
        x2d = _merge(layer, x2d, yt, cb, ccx, w, seq)
        x2d = _ffn(layer, x2d, p3d, w, seq)
    return x2d.reshape(batch, seq, D_MODEL)
```

```python
import functools

import numpy as np
import jax
import jax.numpy as jnp
from jax.experimental import pallas as pl
from jax.experimental.pallas import tpu as pltpu

D_MODEL = 1024
N_HEADS = 8
D_NOPE = 64
D_ROPE = 32
D_QK = D_NOPE + D_ROPE
D_V = 64
Q_LORA = 256
KV_LORA = 128
CONV_WIDTH = 512
D_FF = 2816
PLE_DIM = 256
EPS = 1e-6
ROPE_THETA = 10000.0

HEAD_PAD = 128
HALF_ROPE = D_ROPE // 2
V_ROWS = D_V + 16
N_LATENT = Q_LORA + KV_LORA + D_ROPE
LATENT_COLS = 512
N_CONV_IN = 3 * CONV_WIDTH
N_GATES = 2 * D_MODEL
W_IN_WINDOW = 2048
assert N_LATENT + N_CONV_IN <= W_IN_WINDOW
WEIGHT_PREP_ROWS = 128
HALO = 8
FF_CHUNK = 256
N_FF_CHUNKS = D_FF // FF_CHUNK
DOWN_SPLITS = (N_FF_CHUNKS,)
VMEM_LIMIT_BYTES = 56 * 1024 * 1024

ROW_TILE = 1024
Q_TILE = 512
Q_PER_ROW_TILE = ROW_TILE // Q_TILE
FFN_TILE = 1024
KEY_CHUNK = 256
HEADS_PER_STEP = 8
LOG2_E = 1.4426950408889634

BF16 = jnp.bfloat16
F32 = jnp.float32


def _dot(a, b):
    return jnp.dot(a, b, preferred_element_type=F32)


def _rms_rows(v, gain):
    return v * jax.lax.rsqrt(jnp.mean(v * v, axis=-1, keepdims=True) + EPS) * gain


def _rms_cols(v, gain):
    return v * jax.lax.rsqrt(jnp.mean(v * v, axis=0, keepdims=True) + EPS) * gain


def _rope_cols(t, cos, sin):
    t1, t2 = t[:HALF_ROPE], t[HALF_ROPE:]
    return t1 * cos - t2 * sin, t2 * cos + t1 * sin


def _cut_columns(dst_ref, src_ref, first_col):
    n_rows, width = dst_ref.shape

    def band(r, carry):
        rows = pl.ds(pl.multiple_of(r * WEIGHT_PREP_ROWS, WEIGHT_PREP_ROWS), WEIGHT_PREP_ROWS)
        words = pltpu.bitcast(src_ref[rows, :], jnp.uint32)
        dst_ref[rows, :] = pltpu.bitcast(words[:, first_col:first_col + width], BF16)
        return carry

    jax.lax.fori_loop(0, n_rows // WEIGHT_PREP_ROWS, band, 0)


def _layer_spec(layer, rows, cols, col_block=0):
    return pl.BlockSpec((None, rows, cols), lambda *_: (layer, 0, col_block),
                        pipeline_mode=pl.Buffered(1))


def _params(n_axes, flags=None):
    return pltpu.CompilerParams(
        dimension_semantics=("arbitrary",) * n_axes,
        vmem_limit_bytes=VMEM_LIMIT_BYTES,
        flags=flags)


def _rope_table_kernel(pos_ref, inv_ref, cos_ref, sin_ref):
    ang = pos_ref[...].astype(F32) * inv_ref[...]
    cos_ref[...] = jnp.cos(ang)
    sin_ref[...] = jnp.sin(ang)


def _rope_tables(positions, n_tokens, seq):
    inv = ROPE_THETA ** (-np.arange(0, D_ROPE, 2, dtype=np.float32) / D_ROPE)
    inv = jnp.asarray(inv.astype(np.float32)).reshape(HALF_ROPE, 1)
    pos = positions.reshape(1, n_tokens)
    out = jax.ShapeDtypeStruct((HALF_ROPE, n_tokens), F32)
    return pl.pallas_call(
        _rope_table_kernel,
        grid=(n_tokens // seq,),
        in_specs=[pl.BlockSpec((1, seq), lambda i: (0, i)),
                  pl.BlockSpec((HALF_ROPE, 1), lambda i: (0, 0))],
        out_specs=[pl.BlockSpec((HALF_ROPE, seq), lambda i: (0, i))] * 2,
        out_shape=[out, out],
        compiler_params=_params(1),
        name="rope_tables",
    )(pos, inv)


def _inproj_kernel(x_ref, gmix_ref, win_ref, gq_ref, wuq_ref, gkv_ref,
                   wukv_ref, gqh_ref, gkh_ref, cos_ref, sin_ref,
                   qt_ref, k_ref, vt_ref, cb_ref, ccx_ref, wlat_ref, wconv_in_ref):
    @pl.when(pl.program_id(0) == 0)
    def _():
        _cut_columns(wlat_ref, win_ref, 0)
        _cut_columns(wconv_in_ref, win_ref, N_LATENT)

    h = _rms_rows(x_ref[...], gmix_ref[...]).astype(BF16)

    zt = _dot(h, wlat_ref[...]).T

    zc = _dot(h, wconv_in_ref[...])
    cb_ref[...] = zc[:, :CONV_WIDTH].astype(BF16)
    ccx_ref[...] = (zc[:, CONV_WIDTH:2 * CONV_WIDTH] * zc[:, 2 * CONV_WIDTH:]).astype(BF16)

    cos = cos_ref[...]
    sin = sin_ref[...]

    cqn = _rms_cols(zt[:Q_LORA], gq_ref[...]).astype(BF16)
    qt = _dot(wuq_ref[...], cqn)
    gqh = gqh_ref[...]
    for hd in range(N_HEADS):
        blk = qt[hd * HEAD_PAD:(hd + 1) * HEAD_PAD]
        qn = _rms_cols(blk[:D_NOPE], gqh[:D_NOPE])
        qr = _rms_cols(blk[D_NOPE:D_QK], gqh[D_NOPE:])
        r1, r2 = _rope_cols(qr, cos, sin)
        full = jnp.concatenate([qn, r1, r2, blk[D_QK:]], axis=0)
        full = full.astype(BF16)
        for j in range(Q_PER_ROW_TILE):
            qt_ref[j, hd] = full[:, j * Q_TILE:(j + 1) * Q_TILE]

    ckvn = _rms_cols(zt[Q_LORA:Q_LORA + KV_LORA], gkv_ref[...]).astype(BF16)
    kvt = _dot(wukv_ref[...], ckvn)
    for hd in range(N_HEADS):
        lo = N_HEADS * D_NOPE + hd * D_V
        vt_ref[hd, 0:D_V, :] = kvt[lo:lo + D_V].astype(BF16)
        vt_ref[hd, D_V:, :] = jnp.ones((V_ROWS - D_V, kvt.shape[1]), BF16)

    gkh = gkh_ref[...]
    kr = _rms_cols(zt[Q_LORA + KV_LORA:N_LATENT], gkh[D_NOPE:])
    kr1, kr2 = _rope_cols(kr, cos, sin)
    pad = jnp.zeros((HEAD_PAD - D_QK, kr.shape[1]), F32)
    for hd in range(N_HEADS):
        kn = _rms_cols(kvt[hd * D_NOPE:(hd + 1) * D_NOPE], gkh[:D_NOPE])
        full = jnp.concatenate([kn, kr1, kr2, pad], axis=0)
        k_ref[hd] = full.T.astype(BF16)


def _inproj(layer, x2d, tabs, w):
    n_tokens = x2d.shape[0]
    tm = ROW_TILE
    cos_t, sin_t = tabs
    row = lambda width: pl.BlockSpec((tm, width), lambda i: (i, 0))
    col = lambda height: pl.BlockSpec((height, tm), lambda i: (0, i))
    outs = [
        jax.ShapeDtypeStruct((n_tokens // Q_TILE, N_HEADS, HEAD_PAD, Q_TILE), BF16),
        jax.ShapeDtypeStruct((N_HEADS, n_tokens, HEAD_PAD), BF16),
        jax.ShapeDtypeStruct((N_HEADS, V_ROWS, n_tokens), BF16),
        jax.ShapeDtypeStruct((n_tokens, CONV_WIDTH), BF16),
        jax.ShapeDtypeStruct((n_tokens, CONV_WIDTH), BF16),
    ]
    return pl.pallas_call(
        _inproj_kernel,
        grid=(n_tokens // tm,),
        in_specs=[row(D_MODEL),
                  _layer_spec(layer, 1, D_MODEL),
                  _layer_spec(layer, D_MODEL, W_IN_WINDOW),
                  _layer_spec(layer, Q_LORA, 1),
                  _layer_spec(layer, N_HEADS * HEAD_PAD, Q_LORA),
                  _layer_spec(layer, KV_LORA, 1),
                  _layer_spec(layer, N_HEADS * (D_NOPE + D_V), KV_LORA),
                  _layer_spec(layer, D_QK, 1),
                  _layer_spec(layer, D_QK, 1),
                  col(HALF_ROPE), col(HALF_ROPE)],
        out_specs=[pl.BlockSpec((Q_PER_ROW_TILE, N_HEADS, HEAD_PAD, Q_TILE), lambda i: (i, 0, 0, 0)),
                   pl.BlockSpec((N_HEADS, tm, HEAD_PAD), lambda i: (0, i, 0)),
                   pl.BlockSpec((N_HEADS, V_ROWS, tm), lambda i: (0, 0, i)),
                   row(CONV_WIDTH), row(CONV_WIDTH)],
        out_shape=outs,
        scratch_shapes=[pltpu.VMEM((D_MODEL, LATENT_COLS), BF16),
                        pltpu.VMEM((D_MODEL, N_CONV_IN), BF16)],
        compiler_params=_params(1),
        name="inproj",
    )(x2d, w["g_mix"], w["w_in"], w["g_q_lat"], w["w_uq_t"], w["g_kv_lat"],
      w["w_ukv_t"], w["g_q_head"], w["g_k_head"], cos_t, sin_t)


def _attention_kernel(qt_ref, k_ref, vt_ref, yt_ref, s0_ref, s1_ref):
    nq, heads = qt_ref.shape[:2]
    s_refs = (s0_ref, s1_ref)

    seq = k_ref.shape[1]
    n_chunks = seq // KEY_CHUNK
    rows = lambda c: pl.ds(c * KEY_CHUNK, KEY_CHUNK)

    def score_chunk(q, hd, c, m):
        s = _dot(k_ref[hd, rows(c), :], q)
        s_refs[hd % 2][rows(c), :] = s
        part = jnp.max(s, axis=0, keepdims=True)
        return part if m is None else jnp.maximum(m, part)

    def value_chunk(hd, c, m, o):
        p = jnp.exp2(s_refs[hd % 2][rows(c), :] - m).astype(BF16)
        part = _dot(vt_ref[hd, :, rows(c)], p)
        return part if o is None else o + part

    def emit(qi, hd, o):
        l = o[D_V:D_V + 1]
        yt_ref[qi, hd] = (o[:D_V] * (1.0 / l)).astype(BF16)

    def scores(qi, hd):
        q = qt_ref[qi, hd]
        m = None
        for c in range(n_chunks):
            m = score_chunk(q, hd, c, m)
        return m

    def overlapped(qi_scores, hd_scores, qi_values, hd_values, m_values):
        q = qt_ref[qi_scores, hd_scores]
        m, o = None, None
        for c in range(n_chunks):
            m = score_chunk(q, hd_scores, c, m)
            o = value_chunk(hd_values, c, m_values, o)
        emit(qi_values, hd_values, o)
        return m

    def step(qi, m):
        for hd in range(heads - 1):
            m = overlapped(qi, hd + 1, qi, hd, m)
        return overlapped(qi + 1, 0, qi, heads - 1, m)

    m = jax.lax.fori_loop(0, nq - 1, step, scores(0, 0))
    for hd in range(heads - 1):
        m = overlapped(nq - 1, hd + 1, nq - 1, hd, m)
    o = None
    for c in range(n_chunks):
        o = value_chunk(heads - 1, c, m, o)
    emit(nq - 1, heads - 1, o)


def _attention(qt, k, vt, batch, seq):
    tq = Q_TILE
    nq = seq // tq
    n_groups = N_HEADS // HEADS_PER_STEP
    return pl.pallas_call(
        _attention_kernel,
        grid=(batch, n_groups),
        in_specs=[pl.BlockSpec((nq, HEADS_PER_STEP, HEAD_PAD, tq), lambda b, g: (b, g, 0, 0)),
                  pl.BlockSpec((HEADS_PER_STEP, seq, HEAD_PAD), lambda b, g: (g, b, 0)),
                  pl.BlockSpec((HEADS_PER_STEP, V_ROWS, seq), lambda b, g: (g, 0, b))],
        out_specs=pl.BlockSpec((nq, HEADS_PER_STEP, D_V, tq), lambda b, g: (b, g, 0, 0)),
        out_shape=jax.ShapeDtypeStruct((batch * nq, N_HEADS, D_V, tq), BF16),
        scratch_shapes=[pltpu.VMEM((seq, tq), F32)] * 2,
        compiler_params=_params(2),
        name="attention",
    )(qt, k, vt)


def _conv3(buf_ref, w, n_rows):
    u = buf_ref[...]
    total = u.shape[0]
    full = (pltpu.roll(w[0:1] * u, 1, axis=0)
            + w[1:2] * u
            + pltpu.roll(w[2:3] * u, total - 1, axis=0))
    return full[HALO:HALO + n_rows]


def _edge_masks(tiles_per_seq):
    j = pl.program_id(0) % tiles_per_seq
    return (j != 0).astype(F32), (j != tiles_per_seq - 1).astype(F32)


def _merge_kernel(tiles_per_seq, x_ref, yt_ref, cb_ref, ccx_ref, ccx_prev_ref, ccx_next_ref,
                  gmix_ref, win_ref, bgate_ref, wconv_ref, wau_ref, wcu_ref, wo_ref,
                  out_ref, cs_ref, wgate_ref):
    tm = x_ref.shape[0]

    @pl.when(pl.program_id(0) == 0)
    def _():
        _cut_columns(wgate_ref, win_ref, N_LATENT + N_CONV_IN)

    keep_prev, keep_next = _edge_masks(tiles_per_seq)
    x = x_ref[...]
    h = _rms_rows(x, gmix_ref[...]).astype(BF16)

    two = 2 * HALO
    cs_ref[0:HALO, :] = ccx_prev_ref[...].astype(F32)[two - HALO:] * keep_prev
    cs_ref[HALO:HALO + tm, :] = ccx_ref[...].astype(F32)
    cs_ref[HALO + tm:, :] = ccx_next_ref[...].astype(F32)[:HALO] * keep_next
    y_conv = (cb_ref[...].astype(F32) * _conv3(cs_ref, wconv_ref[...], tm)).astype(BF16)

    y_attn = jnp.concatenate(
        [yt_ref[j].reshape(N_HEADS * D_V, Q_TILE).astype(F32).T for j in range(Q_PER_ROW_TILE)],
        axis=0).astype(BF16)
    gates = jax.nn.sigmoid(_dot(h, wgate_ref[...]) + bgate_ref[...])
    merged = (gates[:, :D_MODEL] * _dot(y_attn, wau_ref[...])
              + gates[:, D_MODEL:] * _dot(y_conv, wcu_ref[...]))
    out_ref[...] = x + _dot(merged.astype(BF16), wo_ref[...])


def _merge(layer, x2d, yt, cb, ccx, w, seq):
    n_tokens = x2d.shape[0]
    tm = ROW_TILE
    two = 2 * HALO
    per_tile = tm // two
    last_blk = n_tokens // two - 1
    row = lambda width: pl.BlockSpec((tm, width), lambda i: (i, 0))
    return pl.pallas_call(
        functools.partial(_merge_kernel, seq // tm),
        grid=(n_tokens // tm,),
        in_specs=[row(D_MODEL),
                  pl.BlockSpec((Q_PER_ROW_TILE, N_HEADS, D_V, Q_TILE), lambda i: (i, 0, 0, 0)),
                  row(CONV_WIDTH), row(CONV_WIDTH),
                  pl.BlockSpec((two, CONV_WIDTH), lambda i: (jnp.maximum(i * per_tile - 1, 0), 0)),
                  pl.BlockSpec((two, CONV_WIDTH),
                               lambda i: (jnp.minimum((i + 1) * per_tile, last_blk), 0)),
                  _layer_spec(layer, 1, D_MODEL),
                  _layer_spec(layer, D_MODEL, N_LATENT + N_CONV_IN + N_GATES),
                  _layer_spec(layer, 1, N_GATES),
                  _layer_spec(layer, 3, CONV_WIDTH),
                  _layer_spec(layer, N_HEADS * D_V, D_MODEL),
                  _layer_spec(layer, CONV_WIDTH, D_MODEL),
                  _layer_spec(layer, D_MODEL, D_MODEL)],
        out_specs=row(D_MODEL),
        out_shape=jax.ShapeDtypeStruct((n_tokens, D_MODEL), F32),
        scratch_shapes=[pltpu.VMEM((tm + 2 * HALO, CONV_WIDTH), F32),
                        pltpu.VMEM((D_MODEL, N_GATES), BF16)],
        compiler_params=_params(1),
        name="merge",
    )(x2d, yt, cb, ccx, ccx, ccx, w["g_mix"], w["w_in"], w["b_gate"], w["w_conv"],
      w["w_attn_up"], w["w_conv_up"], w["w_o"])


def _ffn_kernel(tiles_per_seq, x_ref, xprev_ref, xnext_ref, p_ref, gffn_ref, wup_ref, wfc_ref,
                wdn_ref, gple_ref, wpg_ref, wple_ref, out_ref, lhs_ref, ua0_ref, uv0_ref,
                ua1_ref, uv1_ref, act_ref):
    tm = x_ref.shape[0]
    keep_prev, keep_next = _edge_masks(tiles_per_seq)
    x = x_ref[...]
    gffn = gffn_ref[...]
    lhs_ref[...] = jnp.concatenate(
        [_rms_rows(xprev_ref[...], gffn) * keep_prev,
         _rms_rows(x, gffn),
         _rms_rows(xnext_ref[...], gffn) * keep_next], axis=0).astype(BF16)
    u_bufs = ((ua0_ref, uv0_ref), (ua1_ref, uv1_ref))

    def cols(j, half):
        return pl.ds(half * D_FF + j * FF_CHUNK, FF_CHUNK)

    def up(j):
        lhs = lhs_ref[...]
        ua_ref, uv_ref = u_bufs[j % 2]
        ua_ref[...] = _dot(lhs, wup_ref[:, cols(j, 0)])
        uv_ref[...] = _dot(lhs, wup_ref[:, cols(j, 1)])

    def gate(j):
        ua_ref, uv_ref = u_bufs[j % 2]
        a = _conv3(ua_ref, wfc_ref[:, cols(j, 0)], tm)
        v = _conv3(uv_ref, wfc_ref[:, cols(j, 1)], tm)
        act_ref[:, cols(j, 0)] = (a * jax.nn.sigmoid(a) * v).astype(BF16)

    def down(first, last):
        span = pl.ds(first * FF_CHUNK, (last - first) * FF_CHUNK)
        return _dot(act_ref[:, span], wdn_ref[span, :])

    y = None
    up(0)
    for j in range(N_FF_CHUNKS):
        if j + 1 < N_FF_CHUNKS:
            up(j + 1)
        gate(j)
        if j + 1 in DOWN_SPLITS:
            first = ([0] + list(DOWN_SPLITS))[DOWN_SPLITS.index(j + 1)]
            part = down(first, j + 1)
            y = part if y is None else y + part

    x2 = x + y
    h3 = _rms_rows(x2, gple_ref[...]).astype(BF16)
    pg = jax.nn.sigmoid(_dot(h3, wpg_ref[...]))
    out_ref[...] = x2 + pg * _dot(p_ref[...].astype(BF16), wple_ref[...])


def _ffn(layer, x2d, p3d, w, seq):
    n_tokens = x2d.shape[0]
    tm = FFN_TILE
    per_tile = tm // HALO
    last_blk = n_tokens // HALO - 1
    row = lambda width: pl.BlockSpec((tm, width), lambda i: (i, 0))
    return pl.pallas_call(
        functools.partial(_ffn_kernel, seq // tm),
        grid=(n_tokens // tm,),
        in_specs=[row(D_MODEL),
                  pl.BlockSpec((HALO, D_MODEL), lambda i: (jnp.maximum(i * per_tile - 1, 0), 0)),
                  pl.BlockSpec((HALO, D_MODEL),
                               lambda i: (jnp.minimum((i + 1) * per_tile, last_blk), 0)),
                  pl.BlockSpec((None, tm, PLE_DIM), lambda i: (layer, i, 0)),
                  _layer_spec(layer, 1, D_MODEL),
                  _layer_spec(layer, D_MODEL, 2 * D_FF),
                  _layer_spec(layer, 3, 2 * D_FF),
                  _layer_spec(layer, D_FF, D_MODEL),
                  _layer_spec(layer, 1, D_MODEL),
                  _layer_spec(layer, D_MODEL, D_MODEL),
                  _layer_spec(layer, PLE_DIM, D_MODEL)],
        out_specs=row(D_MODEL),
        out_shape=jax.ShapeDtypeStruct((n_tokens, D_MODEL), F32),
        scratch_shapes=[pltpu.VMEM((tm + 2 * HALO, D_MODEL), BF16)]
        + [pltpu.VMEM((tm + 2 * HALO, FF_CHUNK), F32)] * 4
        + [pltpu.VMEM((tm, D_FF), BF16)],
        compiler_params=_params(1),
        name="ffn",
    )(x2d, x2d, x2d, p3d, w["g_ffn"], w["w_up"], w["w_ffn_conv"], w["w_down"], w["g_ple"],
      w["w_ple_gate"], w["w_ple"])


def _prep_weights(w_in, b_gate, g_mix, g_q_lat, w_uq, g_kv_lat, w_ukv, g_q_head, g_k_head,
                  w_attn_up, w_conv, w_conv_up, w_o, g_ffn, w_up, w_ffn_conv, w_down, g_ple,
                  w_ple_gate, w_ple):
    depth = w_in.shape[0]
    as_row = lambda g: g.reshape(depth, 1, g.shape[-1])
    as_col = lambda g: g.reshape(depth, g.shape[-1], 1)


    uq = w_uq.reshape(depth, Q_LORA, N_HEADS, D_QK)
    uq = jnp.pad(uq, ((0, 0), (0, 0), (0, 0), (0, HEAD_PAD - D_QK)))
    w_uq_t = uq.reshape(depth, Q_LORA, N_HEADS * HEAD_PAD).transpose(0, 2, 1).astype(BF16)

    ukv = w_ukv.reshape(depth, KV_LORA, N_HEADS, D_NOPE + D_V)
    uk = ukv[..., :D_NOPE].reshape(depth, KV_LORA, N_HEADS * D_NOPE)
    uv = ukv[..., D_NOPE:].reshape(depth, KV_LORA, N_HEADS * D_V)
    w_ukv_t = jnp.concatenate([uk, uv], axis=2).transpose(0, 2, 1).astype(BF16)

    return {
        "g_mix": as_row(g_mix),
        "w_in": w_in.astype(BF16),
        "b_gate": as_row(b_gate),
        "g_q_lat": as_col(g_q_lat),
        "w_uq_t": w_uq_t,
        "g_kv_lat": as_col(g_kv_lat),
        "w_ukv_t": w_ukv_t,
        "g_q_head": as_col(g_q_head * (D_QK ** -0.5 * LOG2_E)),
        "g_k_head": as_col(g_k_head),
        "w_attn_up": w_attn_up.astype(BF16),
        "w_conv": w_conv,
        "w_conv_up": w_conv_up.astype(BF16),
        "w_o": w_o.astype(BF16),
        "g_ffn": as_row(g_ffn),
        "w_up": w_up.astype(BF16),
        "w_ffn_conv": w_ffn_conv,
        "w_down": w_down.astype(BF16),
        "g_ple": as_row(g_ple),
        "w_ple_gate": w_ple_gate.astype(BF16),
        "w_ple": w_ple.astype(BF16),
    }


def kernel(x, p, positions, w_in, b_gate, g_mix, g_q_lat, w_uq, g_kv_lat, w_ukv, g_q_head,
           g_k_head, w_attn_up, w_conv, w_conv_up, w_o, g_ffn, w_up, w_ffn_conv, w_down,
           g_ple, w_ple_gate, w_ple):
    batch, seq, d_model = x.shape
    depth = w_in.shape[0]
    assert d_model == D_MODEL and seq % ROW_TILE == 0 and seq % FFN_TILE == 0
    n_tokens = batch * seq
    tabs = _rope_tables(positions, n_tokens, seq)
    w = _prep_weights(w_in, b_gate, g_mix, g_q_lat, w_uq, g_kv_lat, w_ukv, g_q_head, g_k_head,
                      w_attn_up, w_conv, w_conv_up, w_o, g_ffn, w_up, w_ffn_conv, w_down,
                      g_ple, w_ple_gate, w_ple)
    x2d = x.reshape(n_tokens, D_MODEL)
    p3d = p.reshape(depth, n_tokens, PLE_DIM)
    for layer in range(depth):
        qt, k, vt, cb, ccx = _inproj(layer, x2d, tabs, w)
        yt = _attention(qt, k, vt, batch, seq)
        x2d = _merge(layer, x2d, yt, cb, ccx, w, seq)
        x2d = _ffn(layer, x2d, p3d, w, seq)
    return x2d.reshape(batch, seq, D_MODEL)
```

```python
import functools

import numpy as np
import jax
import jax.numpy as jnp
from jax.experimental import pallas as pl
from jax.experimental.pallas import tpu as pltpu

D_MODEL = 1024
N_HEADS = 8
D_NOPE = 64
D_ROPE = 32
D_QK = D_NOPE + D_ROPE
D_V = 64
Q_LORA = 256
KV_LORA = 128
CONV_WIDTH = 512
D_FF = 2816
PLE_DIM = 256
EPS = 1e-6
ROPE_THETA = 10000.0

HEAD_PAD = 128
HALF_ROPE = D_ROPE // 2
V_ROWS = D_V + 16
N_LATENT = Q_LORA + KV_LORA + D_ROPE
LATENT_COLS = 512
N_CONV_IN = 3 * CONV_WIDTH
N_GATES = 2 * D_MODEL
W_IN_WINDOW = 2048
assert N_LATENT + N_CONV_IN <= W_IN_WINDOW
WEIGHT_PREP_ROWS = 128
HALO = 8
BF16_SUBLANES = 16
FF_CHUNK = 256
N_FF_CHUNKS = D_FF // FF_CHUNK
DOWN_SPLITS = (N_FF_CHUNKS,)
VMEM_LIMIT_BYTES = 56 * 1024 * 1024

ROW_TILE = 1024
Q_TILE = 512
Q_PER_ROW_TILE = ROW_TILE // Q_TILE
FFN_TILE = 1024
KEY_CHUNK = 256
HEADS_PER_STEP = 8
LOG2_E = 1.4426950408889634

BF16 = jnp.bfloat16
F32 = jnp.float32


def _dot(a, b):
    return jnp.dot(a, b, preferred_element_type=F32)


def _rms_rows(v, gain):
    return v * jax.lax.rsqrt(jnp.mean(v * v, axis=-1, keepdims=True) + EPS) * gain


def _rms_cols(v, gain):
    return v * jax.lax.rsqrt(jnp.mean(v * v, axis=0, keepdims=True) + EPS) * gain


def _rope_cols(t, cos, sin):
    t1, t2 = t[:HALF_ROPE], t[HALF_ROPE:]
    return t1 * cos - t2 * sin, t2 * cos + t1 * sin


def _cut_columns(dst_ref, src_ref, first_col):
    n_rows, width = dst_ref.shape

    def band(r, carry):
        rows = pl.ds(pl.multiple_of(r * WEIGHT_PREP_ROWS, WEIGHT_PREP_ROWS), WEIGHT_PREP_ROWS)
        words = pltpu.bitcast(src_ref[rows, :], jnp.uint32)
        dst_ref[rows, :] = pltpu.bitcast(words[:, first_col:first_col + width], BF16)
        return carry

    jax.lax.fori_loop(0, n_rows // WEIGHT_PREP_ROWS, band, 0)


def _cast_plan(layer, stacks, n_steps, step_of):
    in_specs, out_specs, out_shapes = [], [], []
    for stack in stacks:
        _, n_rows, n_cols = stack.shape
        slab = n_rows // n_steps
        assert slab * n_steps == n_rows and slab % BF16_SUBLANES == 0
        in_specs.append(pl.BlockSpec((None, slab, n_cols),
                                     lambda *idx: (layer, step_of(*idx), 0)))
        out_specs.append(pl.BlockSpec((slab, n_cols), lambda *idx: (step_of(*idx), 0)))
        out_shapes.append(jax.ShapeDtypeStruct((n_rows, n_cols), BF16))
    return in_specs, out_specs, out_shapes


def _cast_slabs(src_refs, dst_refs):
    for src_ref, dst_ref in zip(src_refs, dst_refs):
        dst_ref[...] = src_ref[...].astype(BF16)


def _whole_spec(rows, cols):
    return pl.BlockSpec((rows, cols), lambda *_: (0, 0), pipeline_mode=pl.Buffered(1))


def _layer_spec(layer, rows, cols, col_block=0):
    return pl.BlockSpec((None, rows, cols), lambda *_: (layer, 0, col_block),
                        pipeline_mode=pl.Buffered(1))


def _params(n_axes, flags=None):
    return pltpu.CompilerParams(
        dimension_semantics=("arbitrary",) * n_axes,
        vmem_limit_bytes=VMEM_LIMIT_BYTES,
        flags=flags)


def _rope_table_kernel(pos_ref, inv_ref, cos_ref, sin_ref):
    ang = pos_ref[...].astype(F32) * inv_ref[...]
    cos_ref[...] = jnp.cos(ang)
    sin_ref[...] = jnp.sin(ang)


def _rope_tables(positions, n_tokens, seq):
    inv = ROPE_THETA ** (-np.arange(0, D_ROPE, 2, dtype=np.float32) / D_ROPE)
    inv = jnp.asarray(inv.astype(np.float32)).reshape(HALF_ROPE, 1)
    pos = positions.reshape(1, n_tokens)
    out = jax.ShapeDtypeStruct((HALF_ROPE, n_tokens), F32)
    return pl.pallas_call(
        _rope_table_kernel,
        grid=(n_tokens // seq,),
        in_specs=[pl.BlockSpec((1, seq), lambda i: (0, i)),
                  pl.BlockSpec((HALF_ROPE, 1), lambda i: (0, 0))],
        out_specs=[pl.BlockSpec((HALF_ROPE, seq), lambda i: (0, i))] * 2,
        out_shape=[out, out],
        compiler_params=_params(1),
        name="rope_tables",
    )(pos, inv)


def _inproj_kernel(n_cast, x_ref, gmix_ref, win_ref, gq_ref, wuq_ref, gkv_ref,
                   wukv_ref, gqh_ref, gkh_ref, cos_ref, sin_ref, *refs):
    cast_src, outs = refs[:n_cast], refs[n_cast:]
    qt_ref, k_ref, vt_ref, cb_ref, ccx_ref = outs[:5]
    cast_dst, (wlat_ref, wconv_in_ref) = outs[5:5 + n_cast], outs[5 + n_cast:]
    _cast_slabs(cast_src, cast_dst)

    @pl.when(pl.program_id(0) == 0)
    def _():
        _cut_columns(wlat_ref, win_ref, 0)
        _cut_columns(wconv_in_ref, win_ref, N_LATENT)

    h = _rms_rows(x_ref[...], gmix_ref[...]).astype(BF16)

    zt = _dot(h, wlat_ref[...]).T

    zc = _dot(h, wconv_in_ref[...])
    cb_ref[...] = zc[:, :CONV_WIDTH].astype(BF16)
    ccx_ref[...] = (zc[:, CONV_WIDTH:2 * CONV_WIDTH] * zc[:, 2 * CONV_WIDTH:]).astype(BF16)

    cos = cos_ref[...]
    sin = sin_ref[...]

    cqn = _rms_cols(zt[:Q_LORA], gq_ref[...]).astype(BF16)
    qt = _dot(wuq_ref[...], cqn)
    gqh = gqh_ref[...]
    for hd in range(N_HEADS):
        blk = qt[hd * HEAD_PAD:(hd + 1) * HEAD_PAD]
        qn = _rms_cols(blk[:D_NOPE], gqh[:D_NOPE])
        qr = _rms_cols(blk[D_NOPE:D_QK], gqh[D_NOPE:])
        r1, r2 = _rope_cols(qr, cos, sin)
        full = jnp.concatenate([qn, r1, r2, blk[D_QK:]], axis=0)
        full = full.astype(BF16)
        for j in range(Q_PER_ROW_TILE):
            qt_ref[j, hd] = full[:, j * Q_TILE:(j + 1) * Q_TILE]

    ckvn = _rms_cols(zt[Q_LORA:Q_LORA + KV_LORA], gkv_ref[...]).astype(BF16)
    kvt = _dot(wukv_ref[...], ckvn)
    for hd in range(N_HEADS):
        lo = N_HEADS * D_NOPE + hd * D_V
        vt_ref[hd, 0:D_V, :] = kvt[lo:lo + D_V].astype(BF16)
        vt_ref[hd, D_V:, :] = jnp.ones((V_ROWS - D_V, kvt.shape[1]), BF16)

    gkh = gkh_ref[...]
    kr = _rms_cols(zt[Q_LORA + KV_LORA:N_LATENT], gkh[D_NOPE:])
    kr1, kr2 = _rope_cols(kr, cos, sin)
    pad = jnp.zeros((HEAD_PAD - D_QK, kr.shape[1]), F32)
    for hd in range(N_HEADS):
        kn = _rms_cols(kvt[hd * D_NOPE:(hd + 1) * D_NOPE], gkh[:D_NOPE])
        full = jnp.concatenate([kn, kr1, kr2, pad], axis=0)
        k_ref[hd] = full.T.astype(BF16)


def _inproj(layer, x2d, tabs, w, cast_stacks):
    n_tokens = x2d.shape[0]
    tm = ROW_TILE
    cos_t, sin_t = tabs
    cast_in, cast_out, cast_shapes = _cast_plan(layer, cast_stacks, n_tokens // tm, lambda i: i)
    row = lambda width: pl.BlockSpec((tm, width), lambda i: (i, 0))
    col = lambda height: pl.BlockSpec((height, tm), lambda i: (0, i))
    outs = [
        jax.ShapeDtypeStruct((n_tokens // Q_TILE, N_HEADS, HEAD_PAD, Q_TILE), BF16),
        jax.ShapeDtypeStruct((N_HEADS, n_tokens, HEAD_PAD), BF16),
        jax.ShapeDtypeStruct((N_HEADS, V_ROWS, n_tokens), BF16),
        jax.ShapeDtypeStruct((n_tokens, CONV_WIDTH), BF16),
        jax.ShapeDtypeStruct((n_tokens, CONV_WIDTH), BF16),
    ]
    results = pl.pallas_call(
        functools.partial(_inproj_kernel, len(cast_stacks)),
        grid=(n_tokens // tm,),
        in_specs=[row(D_MODEL),
                  _layer_spec(layer, 1, D_MODEL),
                  _layer_spec(layer, D_MODEL, W_IN_WINDOW),
                  _layer_spec(layer, Q_LORA, 1),
                  _layer_spec(layer, N_HEADS * HEAD_PAD, Q_LORA),
                  _layer_spec(layer, KV_LORA, 1),
                  _layer_spec(layer, N_HEADS * (D_NOPE + D_V), KV_LORA),
                  _layer_spec(layer, D_QK, 1),
                  _layer_spec(layer, D_QK, 1),
                  col(HALF_ROPE), col(HALF_ROPE)] + cast_in,
        out_specs=[pl.BlockSpec((Q_PER_ROW_TILE, N_HEADS, HEAD_PAD, Q_TILE), lambda i: (i, 0, 0, 0)),
                   pl.BlockSpec((N_HEADS, tm, HEAD_PAD), lambda i: (0, i, 0)),
                   pl.BlockSpec((N_HEADS, V_ROWS, tm), lambda i: (0, 0, i)),
                   row(CONV_WIDTH), row(CONV_WIDTH)] + cast_out,
        out_shape=outs + cast_shapes,
        scratch_shapes=[pltpu.VMEM((D_MODEL, LATENT_COLS), BF16),
                        pltpu.VMEM((D_MODEL, N_CONV_IN), BF16)],
        compiler_params=_params(1),
        name="inproj",
    )(x2d, w["g_mix"], w["w_in"], w["g_q_lat"], w["w_uq_t"], w["g_kv_lat"],
      w["w_ukv_t"], w["g_q_head"], w["g_k_head"], cos_t, sin_t, *cast_stacks)
    return results[:5], results[5:]


def _attention_kernel(n_cast, qt_ref, k_ref, vt_ref, *refs):
    cast_src, (yt_ref, *cast_dst) = refs[:n_cast], refs[n_cast:2 * n_cast + 1]
    s0_ref, s1_ref = refs[2 * n_cast + 1:]
    _cast_slabs(cast_src, cast_dst)
    nq, heads = qt_ref.shape[:2]
    s_refs = (s0_ref, s1_ref)

    seq = k_ref.shape[1]
    n_chunks = seq // KEY_CHUNK
    rows = lambda c: pl.ds(c * KEY_CHUNK, KEY_CHUNK)

    def score_chunk(q, hd, c, m):
        s = _dot(k_ref[hd, rows(c), :], q)
        s_refs[hd % 2][rows(c), :] = s
        part = jnp.max(s, axis=0, keepdims=True)
        return part if m is None else jnp.maximum(m, part)

    def value_chunk(hd, c, m, o):
        p = jnp.exp2(s_refs[hd % 2][rows(c), :] - m).astype(BF16)
        part = _dot(vt_ref[hd, :, rows(c)], p)
        return part if o is None else o + part

    def emit(qi, hd, o):
        l = o[D_V:D_V + 1]
        yt_ref[qi, hd] = (o[:D_V] * (1.0 / l)).astype(BF16)

    def scores(qi, hd):
        q = qt_ref[qi, hd]
        m = None
        for c in range(n_chunks):
            m = score_chunk(q, hd, c, m)
        return m

    def overlapped(qi_scores, hd_scores, qi_values, hd_values, m_values):
        q = qt_ref[qi_scores, hd_scores]
        m, o = None, None
        for c in range(n_chunks):
            m = score_chunk(q, hd_scores, c, m)
            o = value_chunk(hd_values, c, m_values, o)
        emit(qi_values, hd_values, o)
        return m

    def step(qi, m):
        for hd in range(heads - 1):
            m = overlapped(qi, hd + 1, qi, hd, m)
        return overlapped(qi + 1, 0, qi, heads - 1, m)

    m = jax.lax.fori_loop(0, nq - 1, step, scores(0, 0))
    for hd in range(heads - 1):
        m = overlapped(nq - 1, hd + 1, nq - 1, hd, m)
    o = None
    for c in range(n_chunks):
        o = value_chunk(heads - 1, c, m, o)
    emit(nq - 1, heads - 1, o)


def _attention(layer, qt, k, vt, cast_stacks, batch, seq):
    tq = Q_TILE
    nq = seq // tq
    n_groups = N_HEADS // HEADS_PER_STEP
    cast_in, cast_out, cast_shapes = _cast_plan(
        layer, cast_stacks, batch * n_groups, lambda b, g: b * n_groups + g)
    outs = pl.pallas_call(
        functools.partial(_attention_kernel, len(cast_stacks)),
        grid=(batch, n_groups),
        in_specs=[pl.BlockSpec((nq, HEADS_PER_STEP, HEAD_PAD, tq), lambda b, g: (b, g, 0, 0)),
                  pl.BlockSpec((HEADS_PER_STEP, seq, HEAD_PAD), lambda b, g: (g, b, 0)),
                  pl.BlockSpec((HEADS_PER_STEP, V_ROWS, seq), lambda b, g: (g, 0, b))] + cast_in,
        out_specs=[pl.BlockSpec((nq, HEADS_PER_STEP, D_V, tq), lambda b, g: (b, g, 0, 0))]
        + cast_out,
        out_shape=[jax.ShapeDtypeStruct((batch * nq, N_HEADS, D_V, tq), BF16)] + cast_shapes,
        scratch_shapes=[pltpu.VMEM((seq, tq), F32)] * 2,
        compiler_params=_params(2),
        name="attention",
    )(qt, k, vt, *cast_stacks)
    return outs[0], outs[1:]


def _conv3(buf_ref, w, n_rows):
    u = buf_ref[...]
    total = u.shape[0]
    full = (pltpu.roll(w[0:1] * u, 1, axis=0)
            + w[1:2] * u
            + pltpu.roll(w[2:3] * u, total - 1, axis=0))
    return full[HALO:HALO + n_rows]


def _edge_masks(tiles_per_seq):
    j = pl.program_id(0) % tiles_per_seq
    return (j != 0).astype(F32), (j != tiles_per_seq - 1).astype(F32)


def _merge_kernel(tiles_per_seq, x_ref, yt_ref, cb_ref, ccx_ref, ccx_prev_ref, ccx_next_ref,
                  gmix_ref, win_ref, bgate_ref, wconv_ref, wau_ref, wcu_ref, wo_ref,
                  out_ref, cs_ref, wgate_ref):
    tm = x_ref.shape[0]

    @pl.when(pl.program_id(0) == 0)
    def _():
        _cut_columns(wgate_ref, win_ref, N_LATENT + N_CONV_IN)

    keep_prev, keep_next = _edge_masks(tiles_per_seq)
    x = x_ref[...]
    h = _rms_rows(x, gmix_ref[...]).astype(BF16)

    two = 2 * HALO
    cs_ref[0:HALO, :] = ccx_prev_ref[...].astype(F32)[two - HALO:] * keep_prev
    cs_ref[HALO:HALO + tm, :] = ccx_ref[...].astype(F32)
    cs_ref[HALO + tm:, :] = ccx_next_ref[...].astype(F32)[:HALO] * keep_next
    y_conv = (cb_ref[...].astype(F32) * _conv3(cs_ref, wconv_ref[...], tm)).astype(BF16)

    y_attn = jnp.concatenate(
        [yt_ref[j].reshape(N_HEADS * D_V, Q_TILE).astype(F32).T for j in range(Q_PER_ROW_TILE)],
        axis=0).astype(BF16)
    gates = jax.nn.sigmoid(_dot(h, wgate_ref[...]) + bgate_ref[...])
    merged = (gates[:, :D_MODEL] * _dot(y_attn, wau_ref[...])
              + gates[:, D_MODEL:] * _dot(y_conv, wcu_ref[...]))
    out_ref[...] = x + _dot(merged.astype(BF16), wo_ref[...])


def _merge(layer, x2d, yt, cb, ccx, w, branch_weights, seq):
    n_tokens = x2d.shape[0]
    tm = ROW_TILE
    two = 2 * HALO
    per_tile = tm // two
    last_blk = n_tokens // two - 1
    row = lambda width: pl.BlockSpec((tm, width), lambda i: (i, 0))
    return pl.pallas_call(
        functools.partial(_merge_kernel, seq // tm),
        grid=(n_tokens // tm,),
        in_specs=[row(D_MODEL),
                  pl.BlockSpec((Q_PER_ROW_TILE, N_HEADS, D_V, Q_TILE), lambda i: (i, 0, 0, 0)),
                  row(CONV_WIDTH), row(CONV_WIDTH),
                  pl.BlockSpec((two, CONV_WIDTH), lambda i: (jnp.maximum(i * per_tile - 1, 0), 0)),
                  pl.BlockSpec((two, CONV_WIDTH),
                               lambda i: (jnp.minimum((i + 1) * per_tile, last_blk), 0)),
                  _layer_spec(layer, 1, D_MODEL),
                  _layer_spec(layer, D_MODEL, N_LATENT + N_CONV_IN + N_GATES),
                  _layer_spec(layer, 1, N_GATES),
                  _layer_spec(layer, 3, CONV_WIDTH),
                  _whole_spec(N_HEADS * D_V, D_MODEL),
                  _whole_spec(CONV_WIDTH, D_MODEL),
                  _whole_spec(D_MODEL, D_MODEL)],
        out_specs=row(D_MODEL),
        out_shape=jax.ShapeDtypeStruct((n_tokens, D_MODEL), F32),
        scratch_shapes=[pltpu.VMEM((tm + 2 * HALO, CONV_WIDTH), F32),
                        pltpu.VMEM((D_MODEL, N_GATES), BF16)],
        compiler_params=_params(1),
        name="merge",
    )(x2d, yt, cb, ccx, ccx, ccx, w["g_mix"], w["w_in"], w["b_gate"], w["w_conv"],
      *branch_weights)


def _ffn_kernel(tiles_per_seq, x_ref, xprev_ref, xnext_ref, p_ref, gffn_ref, wup_ref, wfc_ref,
                wdn_ref, gple_ref, wpg_ref, wple_ref, out_ref, lhs_ref, ua0_ref, uv0_ref,
                ua1_ref, uv1_ref, act_ref):
    tm = x_ref.shape[0]
    out_ref[...] = _dot(p_ref[...].astype(BF16), wple_ref[...])
    keep_prev, keep_next = _edge_masks(tiles_per_seq)
    x = x_ref[...]
    gffn = gffn_ref[...]
    lhs_ref[...] = jnp.concatenate(
        [_rms_rows(xprev_ref[...], gffn) * keep_prev,
         _rms_rows(x, gffn),
         _rms_rows(xnext_ref[...], gffn) * keep_next], axis=0).astype(BF16)
    u_bufs = ((ua0_ref, uv0_ref), (ua1_ref, uv1_ref))

    def cols(j, half):
        return pl.ds(half * D_FF + j * FF_CHUNK, FF_CHUNK)

    def up(j):
        lhs = lhs_ref[...]
        ua_ref, uv_ref = u_bufs[j % 2]
        ua_ref[...] = _dot(lhs, wup_ref[:, cols(j, 0)])
        uv_ref[...] = _dot(lhs, wup_ref[:, cols(j, 1)])

    def gate(j):
        ua_ref, uv_ref = u_bufs[j % 2]
        a = _conv3(ua_ref, wfc_ref[:, cols(j, 0)], tm)
        v = _conv3(uv_ref, wfc_ref[:, cols(j, 1)], tm)
        act_ref[:, cols(j, 0)] = (a * jax.nn.sigmoid(a) * v).astype(BF16)

    def down(first, last):
        span = pl.ds(first * FF_CHUNK, (last - first) * FF_CHUNK)
        return _dot(act_ref[:, span], wdn_ref[span, :])

    y = None
    up(0)
    for j in range(N_FF_CHUNKS):
        if j + 1 < N_FF_CHUNKS:
            up(j + 1)
        gate(j)
        if j + 1 in DOWN_SPLITS:
            first = ([0] + list(DOWN_SPLITS))[DOWN_SPLITS.index(j + 1)]
            part = down(first, j + 1)
            y = part if y is None else y + part

    x2 = x + y
    h3 = _rms_rows(x2, gple_ref[...]).astype(BF16)
    pg = jax.nn.sigmoid(_dot(h3, wpg_ref[...]))
    out_ref[...] = x2 + pg * out_ref[...]


def _ffn(layer, x2d, p3d, w, mixer_weights, seq):
    w_up_b, w_down_b, w_ple_gate_b, w_ple_b = mixer_weights
    n_tokens = x2d.shape[0]
    tm = FFN_TILE
    per_tile = tm // HALO
    last_blk = n_tokens // HALO - 1
    row = lambda width: pl.BlockSpec((tm, width), lambda i: (i, 0))
    return pl.pallas_call(
        functools.partial(_ffn_kernel, seq // tm),
        grid=(n_tokens // tm,),
        in_specs=[row(D_MODEL),
                  pl.BlockSpec((HALO, D_MODEL), lambda i: (jnp.maximum(i * per_tile - 1, 0), 0)),
                  pl.BlockSpec((HALO, D_MODEL),
                               lambda i: (jnp.minimum((i + 1) * per_tile, last_blk), 0)),
                  pl.BlockSpec((None, tm, PLE_DIM), lambda i: (layer, i, 0)),
                  _layer_spec(layer, 1, D_MODEL),
                  _whole_spec(D_MODEL, 2 * D_FF),
                  _layer_spec(layer, 3, 2 * D_FF),
                  _whole_spec(D_FF, D_MODEL),
                  _layer_spec(layer, 1, D_MODEL),
                  _whole_spec(D_MODEL, D_MODEL),
                  _whole_spec(PLE_DIM, D_MODEL)],
        out_specs=row(D_MODEL),
        out_shape=jax.ShapeDtypeStruct((n_tokens, D_MODEL), F32),
        scratch_shapes=[pltpu.VMEM((tm + 2 * HALO, D_MODEL), BF16)]
        + [pltpu.VMEM((tm + 2 * HALO, FF_CHUNK), F32)] * 4
        + [pltpu.VMEM((tm, D_FF), BF16)],
        compiler_params=_params(1),
        name="ffn",
    )(x2d, x2d, x2d, p3d, w["g_ffn"], w_up_b, w["w_ffn_conv"], w_down_b, w["g_ple"],
      w_ple_gate_b, w_ple_b)


def _prep_weights(w_in, b_gate, g_mix, g_q_lat, w_uq, g_kv_lat, w_ukv, g_q_head, g_k_head,
                  w_conv, g_ffn, w_ffn_conv, g_ple):
    depth = w_in.shape[0]
    as_row = lambda g: g.reshape(depth, 1, g.shape[-1])
    as_col = lambda g: g.reshape(depth, g.shape[-1], 1)

    uq = w_uq.reshape(depth, Q_LORA, N_HEADS, D_QK)
    uq = jnp.pad(uq, ((0, 0), (0, 0), (0, 0), (0, HEAD_PAD - D_QK)))
    w_uq_t = uq.reshape(depth, Q_LORA, N_HEADS * HEAD_PAD).transpose(0, 2, 1).astype(BF16)

    ukv = w_ukv.reshape(depth, KV_LORA, N_HEADS, D_NOPE + D_V)
    uk = ukv[..., :D_NOPE].reshape(depth, KV_LORA, N_HEADS * D_NOPE)
    uv = ukv[..., D_NOPE:].reshape(depth, KV_LORA, N_HEADS * D_V)
    w_ukv_t = jnp.concatenate([uk, uv], axis=2).transpose(0, 2, 1).astype(BF16)

    return {
        "g_mix": as_row(g_mix),
        "w_in": w_in.astype(BF16),
        "b_gate": as_row(b_gate),
        "g_q_lat": as_col(g_q_lat),
        "w_uq_t": w_uq_t,
        "g_kv_lat": as_col(g_kv_lat),
        "w_ukv_t": w_ukv_t,
        "g_q_head": as_col(g_q_head * (D_QK ** -0.5 * LOG2_E)),
        "g_k_head": as_col(g_k_head),
        "w_conv": w_conv,
        "g_ffn": as_row(g_ffn),
        "w_ffn_conv": w_ffn_conv,
        "g_ple": as_row(g_ple),
    }


def kernel(x, p, positions, w_in, b_gate, g_mix, g_q_lat, w_uq, g_kv_lat, w_ukv, g_q_head,
           g_k_head, w_attn_up, w_conv, w_conv_up, w_o, g_ffn, w_up, w_ffn_conv, w_down,
           g_ple, w_ple_gate, w_ple):
    batch, seq, d_model = x.shape
    depth = w_in.shape[0]
    assert d_model == D_MODEL and seq % ROW_TILE == 0 and seq % FFN_TILE == 0
    n_tokens = batch * seq
    tabs = _rope_tables(positions, n_tokens, seq)
    w = _prep_weights(w_in, b_gate, g_mix, g_q_lat, w_uq, g_kv_lat, w_ukv, g_q_head, g_k_head,
                      w_conv, g_ffn, w_ffn_conv, g_ple)
    x2d = x.reshape(n_tokens, D_MODEL)
    p3d = p.reshape(depth, n_tokens, PLE_DIM)
    for layer in range(depth):
        (qt, k, vt, cb, ccx), branch_weights = _inproj(
            layer, x2d, tabs, w, (w_attn_up, w_conv_up, w_o))
        yt, mixer_weights = _attention(
            layer, qt, k, vt, (w_up, w_down, w_ple_gate, w_ple), batch, seq)
        x2d = _merge(layer, x2d, yt, cb, ccx, w, branch_weights, seq)
        x2d = _ffn(layer, x2d, p3d, w, mixer_weights, seq)
    return x2d.reshape(batch, seq, D_MODEL)
```

```python
import functools

import numpy as np
import jax
import jax.numpy as jnp
from jax.experimental import pallas as pl
from jax.experimental.pallas import tpu as pltpu

D_MODEL = 1024
N_HEADS = 8
D_NOPE = 64
D_ROPE = 32
D_QK = D_NOPE + D_ROPE
D_V = 64
Q_LORA = 256
KV_LORA = 128
CONV_WIDTH = 512
D_FF = 2816
PLE_DIM = 256
EPS = 1e-6
ROPE_THETA = 10000.0

HEAD_PAD = 128
HALF_ROPE = D_ROPE // 2
V_ROWS = D_V + 16
N_LATENT = Q_LORA + KV_LORA + D_ROPE
LATENT_COLS = 512
N_CONV_IN = 3 * CONV_WIDTH
N_GATES = 2 * D_MODEL
W_IN_WINDOW = 2048
assert N_LATENT + N_CONV_IN <= W_IN_WINDOW
WEIGHT_PREP_ROWS = 128
HALO = 8
BF16_SUBLANES = 16
FF_CHUNK = 256
N_FF_CHUNKS = D_FF // FF_CHUNK
DOWN_SPLITS = (N_FF_CHUNKS,)
VMEM_LIMIT_BYTES = 56 * 1024 * 1024

ROW_TILE = 1024
Q_TILE = 512
Q_PER_ROW_TILE = ROW_TILE // Q_TILE
FFN_TILE = 1024
KEY_CHUNK = 256
HEADS_PER_STEP = 8
LOG2_E = 1.4426950408889634

BF16 = jnp.bfloat16
F32 = jnp.float32


def _dot(a, b):
    return jnp.dot(a, b, preferred_element_type=F32)


def _rms_rows(v, gain):
    return v * jax.lax.rsqrt(jnp.mean(v * v, axis=-1, keepdims=True) + EPS) * gain


def _rms_cols(v, gain):
    return v * jax.lax.rsqrt(jnp.mean(v * v, axis=0, keepdims=True) + EPS) * gain


def _rope_cols(t, cos, sin):
    t1, t2 = t[:HALF_ROPE], t[HALF_ROPE:]
    return t1 * cos - t2 * sin, t2 * cos + t1 * sin


def _cut_columns(dst_ref, src_ref, first_col):
    n_rows, width = dst_ref.shape

    def band(r, carry):
        rows = pl.ds(pl.multiple_of(r * WEIGHT_PREP_ROWS, WEIGHT_PREP_ROWS), WEIGHT_PREP_ROWS)
        words = pltpu.bitcast(src_ref[rows, :], jnp.uint32)
        dst_ref[rows, :] = pltpu.bitcast(words[:, first_col:first_col + width], BF16)
        return carry

    jax.lax.fori_loop(0, n_rows // WEIGHT_PREP_ROWS, band, 0)


def _cast_plan(layer, stacks, n_steps, step_of):
    in_specs, out_specs, out_shapes = [], [], []
    for stack in stacks:
        _, n_rows, n_cols = stack.shape
        slab = n_rows // n_steps
        assert slab * n_steps == n_rows and slab % BF16_SUBLANES == 0
        in_specs.append(pl.BlockSpec((None, slab, n_cols),
                                     lambda *idx: (layer, step_of(*idx), 0)))
        out_specs.append(pl.BlockSpec((slab, n_cols), lambda *idx: (step_of(*idx), 0)))
        out_shapes.append(jax.ShapeDtypeStruct((n_rows, n_cols), BF16))
    return in_specs, out_specs, out_shapes


def _cast_slabs(src_refs, dst_refs):
    for src_ref, dst_ref in zip(src_refs, dst_refs):
        dst_ref[...] = src_ref[...].astype(BF16)


def _whole_spec(rows, cols):
    return pl.BlockSpec((rows, cols), lambda *_: (0, 0), pipeline_mode=pl.Buffered(1))


def _layer_spec(layer, rows, cols, col_block=0):
    return pl.BlockSpec((None, rows, cols), lambda *_: (layer, 0, col_block),
                        pipeline_mode=pl.Buffered(1))


def _params(n_axes, flags=None):
    return pltpu.CompilerParams(
        dimension_semantics=("arbitrary",) * n_axes,
        vmem_limit_bytes=VMEM_LIMIT_BYTES,
        flags=flags)


def _rope_table_kernel(pos_ref, inv_ref, win_ref, cos_ref, sin_ref, win_b_ref):
    ang = pos_ref[...].astype(F32) * inv_ref[...]
    cos_ref[...] = jnp.cos(ang)
    sin_ref[...] = jnp.sin(ang)
    win_b_ref[...] = win_ref[...].astype(BF16)


def _rope_tables(positions, w_in, n_tokens, seq):
    inv = ROPE_THETA ** (-np.arange(0, D_ROPE, 2, dtype=np.float32) / D_ROPE)
    inv = jnp.asarray(inv.astype(np.float32)).reshape(HALF_ROPE, 1)
    pos = positions.reshape(1, n_tokens)
    out = jax.ShapeDtypeStruct((HALF_ROPE, n_tokens), F32)
    n_steps = n_tokens // seq
    cast_in, cast_out, cast_shapes = _cast_plan(0, (w_in,), n_steps, lambda i: i)
    cos_t, sin_t, w_in_b = pl.pallas_call(
        _rope_table_kernel,
        grid=(n_steps,),
        in_specs=[pl.BlockSpec((1, seq), lambda i: (0, i)),
                  pl.BlockSpec((HALF_ROPE, 1), lambda i: (0, 0))] + cast_in,
        out_specs=[pl.BlockSpec((HALF_ROPE, seq), lambda i: (0, i))] * 2 + cast_out,
        out_shape=[out, out] + cast_shapes,
        compiler_params=_params(1),
        name="rope_tables",
    )(pos, inv, w_in)
    return (cos_t, sin_t), w_in_b


def _inproj_kernel(n_cast, x_ref, gmix_ref, win_ref, gq_ref, wuq_ref, gkv_ref,
                   wukv_ref, gqh_ref, gkh_ref, cos_ref, sin_ref, *refs):
    cast_src, outs = refs[:n_cast], refs[n_cast:]
    qt_ref, k_ref, vt_ref, cb_ref, ccx_ref = outs[:5]
    cast_dst, (wlat_ref, wconv_in_ref) = outs[5:5 + n_cast], outs[5 + n_cast:]
    _cast_slabs(cast_src, cast_dst)

    @pl.when(pl.program_id(0) == 0)
    def _():
        _cut_columns(wlat_ref, win_ref, 0)
        _cut_columns(wconv_in_ref, win_ref, N_LATENT)

    h = _rms_rows(x_ref[...], gmix_ref[...]).astype(BF16)

    zt = _dot(h, wlat_ref[...]).T
    cb_ref[...] = _dot(h, wconv_in_ref[:, :CONV_WIDTH]).astype(BF16)

    cqn = _rms_cols(zt[:Q_LORA], gq_ref[...]).astype(BF16)
    qt = _dot(wuq_ref[...], cqn)
    ckvn = _rms_cols(zt[Q_LORA:Q_LORA + KV_LORA], gkv_ref[...]).astype(BF16)
    kvt = _dot(wukv_ref[...], ckvn)

    zcx = _dot(h, wconv_in_ref[:, CONV_WIDTH:])
    ccx_ref[...] = (zcx[:, :CONV_WIDTH] * zcx[:, CONV_WIDTH:]).astype(BF16)

    cos = cos_ref[...]
    sin = sin_ref[...]
    gqh = gqh_ref[...]
    for hd in range(N_HEADS):
        blk = qt[hd * HEAD_PAD:(hd + 1) * HEAD_PAD]
        qn = _rms_cols(blk[:D_NOPE], gqh[:D_NOPE])
        qr = _rms_cols(blk[D_NOPE:D_QK], gqh[D_NOPE:])
        r1, r2 = _rope_cols(qr, cos, sin)
        full = jnp.concatenate([qn, r1, r2, blk[D_QK:]], axis=0)
        full = full.astype(BF16)
        for j in range(Q_PER_ROW_TILE):
            qt_ref[j, hd] = full[:, j * Q_TILE:(j + 1) * Q_TILE]

    for hd in range(N_HEADS):
        lo = N_HEADS * D_NOPE + hd * D_V
        vt_ref[hd, 0:D_V, :] = kvt[lo:lo + D_V].astype(BF16)
        vt_ref[hd, D_V:, :] = jnp.ones((V_ROWS - D_V, kvt.shape[1]), BF16)

    gkh = gkh_ref[...]
    kr = _rms_cols(zt[Q_LORA + KV_LORA:N_LATENT], gkh[D_NOPE:])
    kr1, kr2 = _rope_cols(kr, cos, sin)
    pad = jnp.zeros((HEAD_PAD - D_QK, kr.shape[1]), F32)
    for hd in range(N_HEADS):
        kn = _rms_cols(kvt[hd * D_NOPE:(hd + 1) * D_NOPE], gkh[:D_NOPE])
        full = jnp.concatenate([kn, kr1, kr2, pad], axis=0)
        k_ref[hd] = full.T.astype(BF16)


def _inproj(layer, x2d, tabs, w, w_in_b, cast_stacks):
    n_tokens = x2d.shape[0]
    tm = ROW_TILE
    cos_t, sin_t = tabs
    cast_in, cast_out, cast_shapes = _cast_plan(layer, cast_stacks, n_tokens // tm, lambda i: i)
    row = lambda width: pl.BlockSpec((tm, width), lambda i: (i, 0))
    col = lambda height: pl.BlockSpec((height, tm), lambda i: (0, i))
    outs = [
        jax.ShapeDtypeStruct((n_tokens // Q_TILE, N_HEADS, HEAD_PAD, Q_TILE), BF16),
        jax.ShapeDtypeStruct((N_HEADS, n_tokens, HEAD_PAD), BF16),
        jax.ShapeDtypeStruct((N_HEADS, V_ROWS, n_tokens), BF16),
        jax.ShapeDtypeStruct((n_tokens, CONV_WIDTH), BF16),
        jax.ShapeDtypeStruct((n_tokens, CONV_WIDTH), BF16),
    ]
    results = pl.pallas_call(
        functools.partial(_inproj_kernel, len(cast_stacks)),
        grid=(n_tokens // tm,),
        in_specs=[row(D_MODEL),
                  _layer_spec(layer, 1, D_MODEL),
                  _whole_spec(D_MODEL, W_IN_WINDOW),
                  _layer_spec(layer, Q_LORA, 1),
                  _layer_spec(layer, N_HEADS * HEAD_PAD, Q_LORA),
                  _layer_spec(layer, KV_LORA, 1),
                  _layer_spec(layer, N_HEADS * (D_NOPE + D_V), KV_LORA),
                  _layer_spec(layer, D_QK, 1),
                  _layer_spec(layer, D_QK, 1),
                  col(HALF_ROPE), col(HALF_ROPE)] + cast_in,
        out_specs=[pl.BlockSpec((Q_PER_ROW_TILE, N_HEADS, HEAD_PAD, Q_TILE), lambda i: (i, 0, 0, 0)),
                   pl.BlockSpec((N_HEADS, tm, HEAD_PAD), lambda i: (0, i, 0)),
                   pl.BlockSpec((N_HEADS, V_ROWS, tm), lambda i: (0, 0, i)),
                   row(CONV_WIDTH), row(CONV_WIDTH)] + cast_out,
        out_shape=outs + cast_shapes,
        scratch_shapes=[pltpu.VMEM((D_MODEL, LATENT_COLS), BF16),
                        pltpu.VMEM((D_MODEL, N_CONV_IN), BF16)],
        compiler_params=_params(1),
        name="inproj",
    )(x2d, w["g_mix"], w_in_b, w["g_q_lat"], w["w_uq_t"], w["g_kv_lat"],
      w["w_ukv_t"], w["g_q_head"], w["g_k_head"], cos_t, sin_t, *cast_stacks)
    return results[:5], results[5:]


def _attention_kernel(n_cast, qt_ref, k_ref, vt_ref, *refs):
    cast_src, (yt_ref, *cast_dst) = refs[:n_cast], refs[n_cast:2 * n_cast + 1]
    s0_ref, s1_ref = refs[2 * n_cast + 1:]
    _cast_slabs(cast_src, cast_dst)
    nq, heads = qt_ref.shape[:2]
    s_refs = (s0_ref, s1_ref)

    seq = k_ref.shape[1]
    n_chunks = seq // KEY_CHUNK
    rows = lambda c: pl.ds(c * KEY_CHUNK, KEY_CHUNK)

    def score_chunk(q, hd, c, m):
        s = _dot(k_ref[hd, rows(c), :], q)
        s_refs[hd % 2][rows(c), :] = s
        part = jnp.max(s, axis=0, keepdims=True)
        return part if m is None else jnp.maximum(m, part)

    def value_chunk(hd, c, m, o):
        p = jnp.exp2(s_refs[hd % 2][rows(c), :] - m).astype(BF16)
        part = _dot(vt_ref[hd, :, rows(c)], p)
        return part if o is None else o + part

    def emit(qi, hd, o):
        l = o[D_V:D_V + 1]
        yt_ref[qi, hd] = (o[:D_V] * (1.0 / l)).astype(BF16)

    def scores(qi, hd):
        q = qt_ref[qi, hd]
        m = None
        for c in range(n_chunks):
            m = score_chunk(q, hd, c, m)
        return m

    def overlapped(qi_scores, hd_scores, qi_values, hd_values, m_values):
        q = qt_ref[qi_scores, hd_scores]
        m, o = None, None
        for c in range(n_chunks):
            m = score_chunk(q, hd_scores, c, m)
            o = value_chunk(hd_values, c, m_values, o)
        emit(qi_values, hd_values, o)
        return m

    def step(qi, m):
        for hd in range(heads - 1):
            m = overlapped(qi, hd + 1, qi, hd, m)
        return overlapped(qi + 1, 0, qi, heads - 1, m)

    m = jax.lax.fori_loop(0, nq - 1, step, scores(0, 0))
    for hd in range(heads - 1):
        m = overlapped(nq - 1, hd + 1, nq - 1, hd, m)
    o = None
    for c in range(n_chunks):
        o = value_chunk(heads - 1, c, m, o)
    emit(nq - 1, heads - 1, o)


def _attention(layer, qt, k, vt, cast_stacks, batch, seq):
    tq = Q_TILE
    nq = seq // tq
    n_groups = N_HEADS // HEADS_PER_STEP
    cast_in, cast_out, cast_shapes = _cast_plan(
        layer, cast_stacks, batch * n_groups, lambda b, g: b * n_groups + g)
    outs = pl.pallas_call(
        functools.partial(_attention_kernel, len(cast_stacks)),
        grid=(batch, n_groups),
        in_specs=[pl.BlockSpec((nq, HEADS_PER_STEP, HEAD_PAD, tq), lambda b, g: (b, g, 0, 0)),
                  pl.BlockSpec((HEADS_PER_STEP, seq, HEAD_PAD), lambda b, g: (g, b, 0)),
                  pl.BlockSpec((HEADS_PER_STEP, V_ROWS, seq), lambda b, g: (g, 0, b))] + cast_in,
        out_specs=[pl.BlockSpec((nq, HEADS_PER_STEP, D_V, tq), lambda b, g: (b, g, 0, 0))]
        + cast_out,
        out_shape=[jax.ShapeDtypeStruct((batch * nq, N_HEADS, D_V, tq), BF16)] + cast_shapes,
        scratch_shapes=[pltpu.VMEM((seq, tq), F32)] * 2,
        compiler_params=_params(2),
        name="attention",
    )(qt, k, vt, *cast_stacks)
    return outs[0], outs[1:]


def _conv3(buf_ref, w, n_rows):
    u = buf_ref[...]
    total = u.shape[0]
    full = (pltpu.roll(w[0:1] * u, 1, axis=0)
            + w[1:2] * u
            + pltpu.roll(w[2:3] * u, total - 1, axis=0))
    return full[HALO:HALO + n_rows]


def _edge_masks(tiles_per_seq):
    j = pl.program_id(0) % tiles_per_seq
    return (j != 0).astype(F32), (j != tiles_per_seq - 1).astype(F32)


def _merge_kernel(tiles_per_seq, x_ref, yt_ref, cb_ref, ccx_ref, ccx_prev_ref, ccx_next_ref,
                  gmix_ref, win_ref, bgate_ref, wconv_ref, wau_ref, wcu_ref, wo_ref,
                  out_ref, cs_ref, wgate_ref):
    tm = x_ref.shape[0]

    @pl.when(pl.program_id(0) == 0)
    def _():
        _cut_columns(wgate_ref, win_ref, N_LATENT + N_CONV_IN)

    keep_prev, keep_next = _edge_masks(tiles_per_seq)
    x = x_ref[...]
    h = _rms_rows(x, gmix_ref[...]).astype(BF16)

    two = 2 * HALO
    cs_ref[0:HALO, :] = ccx_prev_ref[...].astype(F32)[two - HALO:] * keep_prev
    cs_ref[HALO:HALO + tm, :] = ccx_ref[...].astype(F32)
    cs_ref[HALO + tm:, :] = ccx_next_ref[...].astype(F32)[:HALO] * keep_next
    y_conv = (cb_ref[...].astype(F32) * _conv3(cs_ref, wconv_ref[...], tm)).astype(BF16)

    y_attn = jnp.concatenate(
        [yt_ref[j].reshape(N_HEADS * D_V, Q_TILE).astype(F32).T for j in range(Q_PER_ROW_TILE)],
        axis=0).astype(BF16)
    gates = jax.nn.sigmoid(_dot(h, wgate_ref[...]) + bgate_ref[...])
    merged = (gates[:, :D_MODEL] * _dot(y_attn, wau_ref[...])
              + gates[:, D_MODEL:] * _dot(y_conv, wcu_ref[...]))
    out_ref[...] = x + _dot(merged.astype(BF16), wo_ref[...])


def _merge(layer, x2d, yt, cb, ccx, w, w_in_b, branch_weights, seq):
    n_tokens = x2d.shape[0]
    tm = ROW_TILE
    two = 2 * HALO
    per_tile = tm // two
    last_blk = n_tokens // two - 1
    row = lambda width: pl.BlockSpec((tm, width), lambda i: (i, 0))
    return pl.pallas_call(
        functools.partial(_merge_kernel, seq // tm),
        grid=(n_tokens // tm,),
        in_specs=[row(D_MODEL),
                  pl.BlockSpec((Q_PER_ROW_TILE, N_HEADS, D_V, Q_TILE), lambda i: (i, 0, 0, 0)),
                  row(CONV_WIDTH), row(CONV_WIDTH),
                  pl.BlockSpec((two, CONV_WIDTH), lambda i: (jnp.maximum(i * per_tile - 1, 0), 0)),
                  pl.BlockSpec((two, CONV_WIDTH),
                               lambda i: (jnp.minimum((i + 1) * per_tile, last_blk), 0)),
                  _layer_spec(layer, 1, D_MODEL),
                  _whole_spec(D_MODEL, N_LATENT + N_CONV_IN + N_GATES),
                  _layer_spec(layer, 1, N_GATES),
                  _layer_spec(layer, 3, CONV_WIDTH),
                  _whole_spec(N_HEADS * D_V, D_MODEL),
                  _whole_spec(CONV_WIDTH, D_MODEL),
                  _whole_spec(D_MODEL, D_MODEL)],
        out_specs=row(D_MODEL),
        out_shape=jax.ShapeDtypeStruct((n_tokens, D_MODEL), F32),
        scratch_shapes=[pltpu.VMEM((tm + 2 * HALO, CONV_WIDTH), F32),
                        pltpu.VMEM((D_MODEL, N_GATES), BF16)],
        compiler_params=_params(1),
        name="merge",
    )(x2d, yt, cb, ccx, ccx, ccx, w["g_mix"], w_in_b, w["b_gate"], w["w_conv"],
      *branch_weights)


def _ffn_kernel(tiles_per_seq, n_cast, x_ref, xprev_ref, xnext_ref, p_ref, gffn_ref, wup_ref,
                wfc_ref, wdn_ref, gple_ref, wpg_ref, wple_ref, *refs):
    cast_src, (out_ref, *cast_dst) = refs[:n_cast], refs[n_cast:2 * n_cast + 1]
    lhs_ref, ua0_ref, uv0_ref, ua1_ref, uv1_ref, act_ref = refs[2 * n_cast + 1:]
    _cast_slabs(cast_src, cast_dst)
    tm = x_ref.shape[0]
    out_ref[...] = _dot(p_ref[...].astype(BF16), wple_ref[...])
    keep_prev, keep_next = _edge_masks(tiles_per_seq)
    x = x_ref[...]
    gffn = gffn_ref[...]
    lhs_ref[...] = jnp.concatenate(
        [_rms_rows(xprev_ref[...], gffn) * keep_prev,
         _rms_rows(x, gffn),
         _rms_rows(xnext_ref[...], gffn) * keep_next], axis=0).astype(BF16)
    u_bufs = ((ua0_ref, uv0_ref), (ua1_ref, uv1_ref))

    def cols(j, half):
        return pl.ds(half * D_FF + j * FF_CHUNK, FF_CHUNK)

    def up(j):
        lhs = lhs_ref[...]
        ua_ref, uv_ref = u_bufs[j % 2]
        ua_ref[...] = _dot(lhs, wup_ref[:, cols(j, 0)])
        uv_ref[...] = _dot(lhs, wup_ref[:, cols(j, 1)])

    def gate(j):
        ua_ref, uv_ref = u_bufs[j % 2]
        a = _conv3(ua_ref, wfc_ref[:, cols(j, 0)], tm)
        v = _conv3(uv_ref, wfc_ref[:, cols(j, 1)], tm)
        act_ref[:, cols(j, 0)] = (a * jax.nn.sigmoid(a) * v).astype(BF16)

    def down(first, last):
        span = pl.ds(first * FF_CHUNK, (last - first) * FF_CHUNK)
        return _dot(act_ref[:, span], wdn_ref[span, :])

    y = None
    up(0)
    for j in range(N_FF_CHUNKS):
        if j + 1 < N_FF_CHUNKS:
            up(j + 1)
        gate(j)
        if j + 1 in DOWN_SPLITS:
            first = ([0] + list(DOWN_SPLITS))[DOWN_SPLITS.index(j + 1)]
            part = down(first, j + 1)
            y = part if y is None else y + part

    x2 = x + y
    h3 = _rms_rows(x2, gple_ref[...]).astype(BF16)
    pg = jax.nn.sigmoid(_dot(h3, wpg_ref[...]))
    out_ref[...] = x2 + pg * out_ref[...]


def _ffn(layer, x2d, p3d, w, mixer_weights, cast_layer, cast_stacks, seq):
    w_up_b, w_down_b, w_ple_gate_b, w_ple_b = mixer_weights
    n_tokens = x2d.shape[0]
    tm = FFN_TILE
    per_tile = tm // HALO
    last_blk = n_tokens // HALO - 1
    row = lambda width: pl.BlockSpec((tm, width), lambda i: (i, 0))
    cast_in, cast_out, cast_shapes = _cast_plan(cast_layer, cast_stacks, n_tokens // tm,
                                                lambda i: i)
    outs = pl.pallas_call(
        functools.partial(_ffn_kernel, seq // tm, len(cast_stacks)),
        grid=(n_tokens // tm,),
        in_specs=[row(D_MODEL),
                  pl.BlockSpec((HALO, D_MODEL), lambda i: (jnp.maximum(i * per_tile - 1, 0), 0)),
                  pl.BlockSpec((HALO, D_MODEL),
                               lambda i: (jnp.minimum((i + 1) * per_tile, last_blk), 0)),
                  pl.BlockSpec((None, tm, PLE_DIM), lambda i: (layer, i, 0)),
                  _layer_spec(layer, 1, D_MODEL),
                  _whole_spec(D_MODEL, 2 * D_FF),
                  _layer_spec(layer, 3, 2 * D_FF),
                  _whole_spec(D_FF, D_MODEL),
                  _layer_spec(layer, 1, D_MODEL),
                  _whole_spec(D_MODEL, D_MODEL),
                  _whole_spec(PLE_DIM, D_MODEL)] + cast_in,
        out_specs=[row(D_MODEL)] + cast_out,
        out_shape=[jax.ShapeDtypeStruct((n_tokens, D_MODEL), F32)] + cast_shapes,
        scratch_shapes=[pltpu.VMEM((tm + 2 * HALO, D_MODEL), BF16)]
        + [pltpu.VMEM((tm + 2 * HALO, FF_CHUNK), F32)] * 4
        + [pltpu.VMEM((tm, D_FF), BF16)],
        compiler_params=_params(1),
        name="ffn",
    )(x2d, x2d, x2d, p3d, w["g_ffn"], w_up_b, w["w_ffn_conv"], w_down_b, w["g_ple"],
      w_ple_gate_b, w_ple_b, *cast_stacks)
    return outs[0], outs[1:]


def _prep_weights(b_gate, g_mix, g_q_lat, w_uq, g_kv_lat, w_ukv, g_q_head, g_k_head,
                  w_conv, g_ffn, w_ffn_conv, g_ple):
    depth = g_mix.shape[0]
    as_row = lambda g: g.reshape(depth, 1, g.shape[-1])
    as_col = lambda g: g.reshape(depth, g.shape[-1], 1)

    uq = w_uq.reshape(depth, Q_LORA, N_HEADS, D_QK)
    uq = jnp.pad(uq, ((0, 0), (0, 0), (0, 0), (0, HEAD_PAD - D_QK)))
    w_uq_t = uq.reshape(depth, Q_LORA, N_HEADS * HEAD_PAD).transpose(0, 2, 1).astype(BF16)

    ukv = w_ukv.reshape(depth, KV_LORA, N_HEADS, D_NOPE + D_V)
    uk = ukv[..., :D_NOPE].reshape(depth, KV_LORA, N_HEADS * D_NOPE)
    uv = ukv[..., D_NOPE:].reshape(depth, KV_LORA, N_HEADS * D_V)
    w_ukv_t = jnp.concatenate([uk, uv], axis=2).transpose(0, 2, 1).astype(BF16)

    return {
        "g_mix": as_row(g_mix),
        "b_gate": as_row(b_gate),
        "g_q_lat": as_col(g_q_lat),
        "w_uq_t": w_uq_t,
        "g_kv_lat": as_col(g_kv_lat),
        "w_ukv_t": w_ukv_t,
        "g_q_head": as_col(g_q_head * (D_QK ** -0.5 * LOG2_E)),
        "g_k_head": as_col(g_k_head),
        "w_conv": w_conv,
        "g_ffn": as_row(g_ffn),
        "w_ffn_conv": w_ffn_conv,
        "g_ple": as_row(g_ple),
    }


def kernel(x, p, positions, w_in, b_gate, g_mix, g_q_lat, w_uq, g_kv_lat, w_ukv, g_q_head,
           g_k_head, w_attn_up, w_conv, w_conv_up, w_o, g_ffn, w_up, w_ffn_conv, w_down,
           g_ple, w_ple_gate, w_ple):
    batch, seq, d_model = x.shape
    depth = w_in.shape[0]
    assert d_model == D_MODEL and seq % ROW_TILE == 0 and seq % FFN_TILE == 0
    n_tokens = batch * seq
    tabs, w_in_b = _rope_tables(positions, w_in, n_tokens, seq)
    w = _prep_weights(b_gate, g_mix, g_q_lat, w_uq, g_kv_lat, w_ukv, g_q_head, g_k_head,
                      w_conv, g_ffn, w_ffn_conv, g_ple)
    x2d = x.reshape(n_tokens, D_MODEL)
    p3d = p.reshape(depth, n_tokens, PLE_DIM)
    for layer in range(depth):
        (qt, k, vt, cb, ccx), branch_weights = _inproj(
            layer, x2d, tabs, w, w_in_b, (w_attn_up, w_conv_up, w_o))
        yt, mixer_weights = _attention(
            layer, qt, k, vt, (w_up, w_down, w_ple_gate, w_ple), batch, seq)
        x2d = _merge(layer, x2d, yt, cb, ccx, w, w_in_b, branch_weights, seq)
        next_stacks = (w_in,) if layer + 1 < depth else ()
        x2d, next_casts = _ffn(layer, x2d, p3d, w, mixer_weights, layer + 1, next_stacks, seq)
        if next_casts:
            (w_in_b,) = next_casts
    return x2d.reshape(batch, seq, D_MODEL)
```

```python
import functools

import numpy as np
import jax
import jax.numpy as jnp
from jax.experimental import pallas as pl
from jax.experimental.pallas import tpu as pltpu

D_MODEL = 1024
N_HEADS = 8
D_NOPE = 64
D_ROPE = 32
D_QK = D_NOPE + D_ROPE
D_V = 64
Q_LORA = 256
KV_LORA = 128
CONV_WIDTH = 512
D_FF = 2816
PLE_DIM = 256
EPS = 1e-6
ROPE_THETA = 10000.0

HEAD_PAD = 128
HALF_ROPE = D_ROPE // 2
V_ROWS = D_V + 16
N_LATENT = Q_LORA + KV_LORA + D_ROPE
LATENT_COLS = 512
N_CONV_IN = 3 * CONV_WIDTH
N_GATES = 2 * D_MODEL
W_IN_WINDOW = 2048
assert N_LATENT + N_CONV_IN <= W_IN_WINDOW
WEIGHT_PREP_ROWS = 128
HALO = 8
BF16_SUBLANES = 16
FF_CHUNK = 256
N_FF_CHUNKS = D_FF // FF_CHUNK
DOWN_SPLITS = (N_FF_CHUNKS,)
VMEM_LIMIT_BYTES = 56 * 1024 * 1024

ROW_TILE = 1024
Q_TILE = 512
Q_PER_ROW_TILE = ROW_TILE // Q_TILE
FFN_TILE = 1024
KEY_CHUNK = 256
HEADS_PER_STEP = 8
LOG2_E = 1.4426950408889634

BF16 = jnp.bfloat16
F32 = jnp.float32


def _dot(a, b):
    return jnp.dot(a, b, preferred_element_type=F32)


def _rms_rows(v, gain):
    return v * jax.lax.rsqrt(jnp.mean(v * v, axis=-1, keepdims=True) + EPS) * gain


def _rms_cols(v, gain):
    return v * jax.lax.rsqrt(jnp.mean(v * v, axis=0, keepdims=True) + EPS) * gain


def _rope_cols(t, cos, sin):
    t1, t2 = t[:HALF_ROPE], t[HALF_ROPE:]
    return t1 * cos - t2 * sin, t2 * cos + t1 * sin


def _cut_columns(dst_ref, src_ref, first_col):
    n_rows, width = dst_ref.shape

    def band(r, carry):
        rows = pl.ds(pl.multiple_of(r * WEIGHT_PREP_ROWS, WEIGHT_PREP_ROWS), WEIGHT_PREP_ROWS)
        words = pltpu.bitcast(src_ref[rows, :], jnp.uint32)
        dst_ref[rows, :] = pltpu.bitcast(words[:, first_col:first_col + width], BF16)
        return carry

    jax.lax.fori_loop(0, n_rows // WEIGHT_PREP_ROWS, band, 0)


def _cast_plan(layer, stacks, n_steps, step_of):
    in_specs, out_specs, out_shapes = [], [], []
    for stack in stacks:
        _, n_rows, n_cols = stack.shape
        slab = n_rows // n_steps
        assert slab * n_steps == n_rows and slab % BF16_SUBLANES == 0
        in_specs.append(pl.BlockSpec((None, slab, n_cols),
                                     lambda *idx: (layer, step_of(*idx), 0)))
        out_specs.append(pl.BlockSpec((slab, n_cols), lambda *idx: (step_of(*idx), 0)))
        out_shapes.append(jax.ShapeDtypeStruct((n_rows, n_cols), BF16))
    return in_specs, out_specs, out_shapes


def _cast_slabs(src_refs, dst_refs):
    for src_ref, dst_ref in zip(src_refs, dst_refs):
        dst_ref[...] = src_ref[...].astype(BF16)


def _whole_spec(rows, cols):
    return pl.BlockSpec((rows, cols), lambda *_: (0, 0), pipeline_mode=pl.Buffered(1))


def _layer_spec(layer, rows, cols):
    return pl.BlockSpec((None, rows, cols), lambda *_: (layer, 0, 0),
                        pipeline_mode=pl.Buffered(1))


def _params(n_axes):
    return pltpu.CompilerParams(
        dimension_semantics=("arbitrary",) * n_axes,
        vmem_limit_bytes=VMEM_LIMIT_BYTES)


def _rope_table_kernel(pos_ref, inv_ref, cos_ref, sin_ref):
    ang = pos_ref[...].astype(F32) * inv_ref[...]
    cos_ref[...] = jnp.cos(ang)
    sin_ref[...] = jnp.sin(ang)


def _rope_tables(positions, n_tokens, seq):
    inv = ROPE_THETA ** (-np.arange(0, D_ROPE, 2, dtype=np.float32) / D_ROPE)
    inv = jnp.asarray(inv.astype(np.float32)).reshape(HALF_ROPE, 1)
    pos = positions.reshape(1, n_tokens)
    out = jax.ShapeDtypeStruct((HALF_ROPE, n_tokens), F32)
    return pl.pallas_call(
        _rope_table_kernel,
        grid=(n_tokens // seq,),
        in_specs=[pl.BlockSpec((1, seq), lambda i: (0, i)),
                  pl.BlockSpec((HALF_ROPE, 1), lambda i: (0, 0))],
        out_specs=[pl.BlockSpec((HALF_ROPE, seq), lambda i: (0, i))] * 2,
        out_shape=[out, out],
        compiler_params=_params(1),
        name="rope_tables",
    )(pos, inv)


def _inproj_kernel(n_cast, x_ref, gmix_ref, win_ref, gq_ref, wuq_ref, gkv_ref,
                   wukv_ref, gqh_ref, gkh_ref, cos_ref, sin_ref, *refs):
    cast_src, outs = refs[:n_cast], refs[n_cast:]
    qt_ref, k_ref, vt_ref, cb_ref, ccx_ref = outs[:5]
    cast_dst, (wlat_ref, wconv_in_ref) = outs[5:5 + n_cast], outs[5 + n_cast:]
    _cast_slabs(cast_src, cast_dst)

    @pl.when(pl.program_id(0) == 0)
    def _():
        _cut_columns(wlat_ref, win_ref, 0)
        _cut_columns(wconv_in_ref, win_ref, N_LATENT)

    h = _rms_rows(x_ref[...], gmix_ref[...]).astype(BF16)

    zt = _dot(h, wlat_ref[...]).T
    cb_ref[...] = _dot(h, wconv_in_ref[:, :CONV_WIDTH]).astype(BF16)

    cqn = _rms_cols(zt[:Q_LORA], gq_ref[...]).astype(BF16)
    qt = _dot(wuq_ref[...], cqn)
    ckvn = _rms_cols(zt[Q_LORA:Q_LORA + KV_LORA], gkv_ref[...]).astype(BF16)
    kvt = _dot(wukv_ref[...], ckvn)

    zcx = _dot(h, wconv_in_ref[:, CONV_WIDTH:])
    ccx_ref[...] = (zcx[:, :CONV_WIDTH] * zcx[:, CONV_WIDTH:]).astype(BF16)

    cos = cos_ref[...]
    sin = sin_ref[...]
    gqh = gqh_ref[...]
    for hd in range(N_HEADS):
        blk = qt[hd * HEAD_PAD:(hd + 1) * HEAD_PAD]
        qn = _rms_cols(blk[:D_NOPE], gqh[:D_NOPE])
        qr = _rms_cols(blk[D_NOPE:D_QK], gqh[D_NOPE:])
        r1, r2 = _rope_cols(qr, cos, sin)
        full = jnp.concatenate([qn, r1, r2, blk[D_QK:]], axis=0)
        full = full.astype(BF16)
        for j in range(Q_PER_ROW_TILE):
            qt_ref[j, hd] = full[:, j * Q_TILE:(j + 1) * Q_TILE]

    for hd in range(N_HEADS):
        lo = N_HEADS * D_NOPE + hd * D_V
        vt_ref[hd, 0:D_V, :] = kvt[lo:lo + D_V].astype(BF16)
        vt_ref[hd, D_V:, :] = jnp.ones((V_ROWS - D_V, kvt.shape[1]), BF16)

    gkh = gkh_ref[...]
    kr = _rms_cols(zt[Q_LORA + KV_LORA:N_LATENT], gkh[D_NOPE:])
    kr1, kr2 = _rope_cols(kr, cos, sin)
    pad = jnp.zeros((HEAD_PAD - D_QK, kr.shape[1]), F32)
    for hd in range(N_HEADS):
        kn = _rms_cols(kvt[hd * D_NOPE:(hd + 1) * D_NOPE], gkh[:D_NOPE])
        full = jnp.concatenate([kn, kr1, kr2, pad], axis=0)
        k_ref[hd] = full.T.astype(BF16)


def _inproj(layer, x2d, tabs, w, cast_stacks):
    n_tokens = x2d.shape[0]
    tm = ROW_TILE
    cos_t, sin_t = tabs
    cast_in, cast_out, cast_shapes = _cast_plan(layer, cast_stacks, n_tokens // tm, lambda i: i)
    row = lambda width: pl.BlockSpec((tm, width), lambda i: (i, 0))
    col = lambda height: pl.BlockSpec((height, tm), lambda i: (0, i))
    outs = [
        jax.ShapeDtypeStruct((n_tokens // Q_TILE, N_HEADS, HEAD_PAD, Q_TILE), BF16),
        jax.ShapeDtypeStruct((N_HEADS, n_tokens, HEAD_PAD), BF16),
        jax.ShapeDtypeStruct((N_HEADS, V_ROWS, n_tokens), BF16),
        jax.ShapeDtypeStruct((n_tokens, CONV_WIDTH), BF16),
        jax.ShapeDtypeStruct((n_tokens, CONV_WIDTH), BF16),
    ]
    results = pl.pallas_call(
        functools.partial(_inproj_kernel, len(cast_stacks)),
        grid=(n_tokens // tm,),
        in_specs=[row(D_MODEL),
                  _layer_spec(layer, 1, D_MODEL),
                  _layer_spec(layer, D_MODEL, W_IN_WINDOW),
                  _layer_spec(layer, Q_LORA, 1),
                  _layer_spec(layer, N_HEADS * HEAD_PAD, Q_LORA),
                  _layer_spec(layer, KV_LORA, 1),
                  _layer_spec(layer, N_HEADS * (D_NOPE + D_V), KV_LORA),
                  _layer_spec(layer, D_QK, 1),
                  _layer_spec(layer, D_QK, 1),
                  col(HALF_ROPE), col(HALF_ROPE)] + cast_in,
        out_specs=[pl.BlockSpec((Q_PER_ROW_TILE, N_HEADS, HEAD_PAD, Q_TILE), lambda i: (i, 0, 0, 0)),
                   pl.BlockSpec((N_HEADS, tm, HEAD_PAD), lambda i: (0, i, 0)),
                   pl.BlockSpec((N_HEADS, V_ROWS, tm), lambda i: (0, 0, i)),
                   row(CONV_WIDTH), row(CONV_WIDTH)] + cast_out,
        out_shape=outs + cast_shapes,
        scratch_shapes=[pltpu.VMEM((D_MODEL, LATENT_COLS), BF16),
                        pltpu.VMEM((D_MODEL, N_CONV_IN), BF16)],
        compiler_params=_params(1),
        name="inproj",
    )(x2d, w["g_mix"], w["w_in"], w["g_q_lat"], w["w_uq_t"], w["g_kv_lat"],
      w["w_ukv_t"], w["g_q_head"], w["g_k_head"], cos_t, sin_t, *cast_stacks)
    return results[:5], results[5:]


def _attention_kernel(n_cast, qt_ref, k_ref, vt_ref, *refs):
    cast_src, (yt_ref, *cast_dst) = refs[:n_cast], refs[n_cast:2 * n_cast + 1]
    s0_ref, s1_ref = refs[2 * n_cast + 1:]
    _cast_slabs(cast_src, cast_dst)
    nq, heads = qt_ref.shape[:2]
    s_refs = (s0_ref, s1_ref)

    seq = k_ref.shape[1]
    n_chunks = seq // KEY_CHUNK
    rows = lambda c: pl.ds(c * KEY_CHUNK, KEY_CHUNK)

    def score_chunk(q, hd, c, m):
        s = _dot(k_ref[hd, rows(c), :], q)
        s_refs[hd % 2][rows(c), :] = s
        part = jnp.max(s, axis=0, keepdims=True)
        return part if m is None else jnp.maximum(m, part)

    def value_chunk(hd, c, m, o):
        p = jnp.exp2(s_refs[hd % 2][rows(c), :] - m).astype(BF16)
        part = _dot(vt_ref[hd, :, rows(c)], p)
        return part if o is None else o + part

    def emit(qi, hd, o):
        l = o[D_V:D_V + 1]
        yt_ref[qi, hd] = (o[:D_V] * (1.0 / l)).astype(BF16)

    def scores(qi, hd):
        q = qt_ref[qi, hd]
        m = None
        for c in range(n_chunks):
            m = score_chunk(q, hd, c, m)
        return m

    def overlapped(qi_scores, hd_scores, qi_values, hd_values, m_values):
        q = qt_ref[qi_scores, hd_scores]
        m, o = None, None
        for c in range(n_chunks):
            m = score_chunk(q, hd_scores, c, m)
            o = value_chunk(hd_values, c, m_values, o)
        emit(qi_values, hd_values, o)
        return m

    def step(qi, m):
        for hd in range(heads - 1):
            m = overlapped(qi, hd + 1, qi, hd, m)
        return overlapped(qi + 1, 0, qi, heads - 1, m)

    m = jax.lax.fori_loop(0, nq - 1, step, scores(0, 0))
    for hd in range(heads - 1):
        m = overlapped(nq - 1, hd + 1, nq - 1, hd, m)
    o = None
    for c in range(n_chunks):
        o = value_chunk(heads - 1, c, m, o)
    emit(nq - 1, heads - 1, o)


def _attention(layer, qt, k, vt, cast_stacks, batch, seq):
    tq = Q_TILE
    nq = seq // tq
    n_groups = N_HEADS // HEADS_PER_STEP
    cast_in, cast_out, cast_shapes = _cast_plan(
        layer, cast_stacks, batch * n_groups, lambda b, g: b * n_groups + g)
    outs = pl.pallas_call(
        functools.partial(_attention_kernel, len(cast_stacks)),
        grid=(batch, n_groups),
        in_specs=[pl.BlockSpec((nq, HEADS_PER_STEP, HEAD_PAD, tq), lambda b, g: (b, g, 0, 0)),
                  pl.BlockSpec((HEADS_PER_STEP, seq, HEAD_PAD), lambda b, g: (g, b, 0)),
                  pl.BlockSpec((HEADS_PER_STEP, V_ROWS, seq), lambda b, g: (g, 0, b))] + cast_in,
        out_specs=[pl.BlockSpec((nq, HEADS_PER_STEP, D_V, tq), lambda b, g: (b, g, 0, 0))]
        + cast_out,
        out_shape=[jax.ShapeDtypeStruct((batch * nq, N_HEADS, D_V, tq), BF16)] + cast_shapes,
        scratch_shapes=[pltpu.VMEM((seq, tq), F32)] * 2,
        compiler_params=_params(2),
        name="attention",
    )(qt, k, vt, *cast_stacks)
    return outs[0], outs[1:]


def _conv3(buf_ref, w, n_rows):
    u = buf_ref[...]
    total = u.shape[0]
    full = (pltpu.roll(w[0:1] * u, 1, axis=0)
            + w[1:2] * u
            + pltpu.roll(w[2:3] * u, total - 1, axis=0))
    return full[HALO:HALO + n_rows]


def _edge_masks(tiles_per_seq):
    j = pl.program_id(0) % tiles_per_seq
    return (j != 0).astype(F32), (j != tiles_per_seq - 1).astype(F32)


def _merge_kernel(tiles_per_seq, x_ref, yt_ref, cb_ref, ccx_ref, ccx_prev_ref, ccx_next_ref,
                  gmix_ref, win_ref, bgate_ref, wconv_ref, wau_ref, wcu_ref, wo_ref,
                  out_ref, cs_ref, wgate_ref):
    tm = x_ref.shape[0]

    @pl.when(pl.program_id(0) == 0)
    def _():
        _cut_columns(wgate_ref, win_ref, N_LATENT + N_CONV_IN)

    keep_prev, keep_next = _edge_masks(tiles_per_seq)
    x = x_ref[...]
    h = _rms_rows(x, gmix_ref[...]).astype(BF16)

    two = 2 * HALO
    cs_ref[0:HALO, :] = ccx_prev_ref[...].astype(F32)[two - HALO:] * keep_prev
    cs_ref[HALO:HALO + tm, :] = ccx_ref[...].astype(F32)
    cs_ref[HALO + tm:, :] = ccx_next_ref[...].astype(F32)[:HALO] * keep_next
    y_conv = (cb_ref[...].astype(F32) * _conv3(cs_ref, wconv_ref[...], tm)).astype(BF16)

    y_attn = jnp.concatenate(
        [yt_ref[j].reshape(N_HEADS * D_V, Q_TILE).astype(F32).T for j in range(Q_PER_ROW_TILE)],
        axis=0).astype(BF16)
    gates = jax.nn.sigmoid(_dot(h, wgate_ref[...]) + bgate_ref[...])
    merged = (gates[:, :D_MODEL] * _dot(y_attn, wau_ref[...])
              + gates[:, D_MODEL:] * _dot(y_conv, wcu_ref[...]))
    out_ref[...] = x + _dot(merged.astype(BF16), wo_ref[...])


def _merge(layer, x2d, yt, cb, ccx, w, branch_weights, seq):
    n_tokens = x2d.shape[0]
    tm = ROW_TILE
    two = 2 * HALO
    per_tile = tm // two
    last_blk = n_tokens // two - 1
    row = lambda width: pl.BlockSpec((tm, width), lambda i: (i, 0))
    return pl.pallas_call(
        functools.partial(_merge_kernel, seq // tm),
        grid=(n_tokens // tm,),
        in_specs=[row(D_MODEL),
                  pl.BlockSpec((Q_PER_ROW_TILE, N_HEADS, D_V, Q_TILE), lambda i: (i, 0, 0, 0)),
                  row(CONV_WIDTH), row(CONV_WIDTH),
                  pl.BlockSpec((two, CONV_WIDTH), lambda i: (jnp.maximum(i * per_tile - 1, 0), 0)),
                  pl.BlockSpec((two, CONV_WIDTH),
                               lambda i: (jnp.minimum((i + 1) * per_tile, last_blk), 0)),
                  _layer_spec(layer, 1, D_MODEL),
                  _layer_spec(layer, D_MODEL, N_LATENT + N_CONV_IN + N_GATES),
                  _layer_spec(layer, 1, N_GATES),
                  _layer_spec(layer, 3, CONV_WIDTH),
                  _whole_spec(N_HEADS * D_V, D_MODEL),
                  _whole_spec(CONV_WIDTH, D_MODEL),
                  _whole_spec(D_MODEL, D_MODEL)],
        out_specs=row(D_MODEL),
        out_shape=jax.ShapeDtypeStruct((n_tokens, D_MODEL), F32),
        scratch_shapes=[pltpu.VMEM((tm + 2 * HALO, CONV_WIDTH), F32),
                        pltpu.VMEM((D_MODEL, N_GATES), BF16)],
        compiler_params=_params(1),
        name="merge",
    )(x2d, yt, cb, ccx, ccx, ccx, w["g_mix"], w["w_in"], w["b_gate"], w["w_conv"],
      *branch_weights)


def _ffn_kernel(tiles_per_seq, x_ref, xprev_ref, xnext_ref, p_ref, gffn_ref, wup_ref, wfc_ref,
                wdn_ref, gple_ref, wpg_ref, wple_ref, out_ref, lhs_ref, ua0_ref, uv0_ref,
                ua1_ref, uv1_ref, act_ref):
    tm = x_ref.shape[0]
    out_ref[...] = _dot(p_ref[...].astype(BF16), wple_ref[...])
    keep_prev, keep_next = _edge_masks(tiles_per_seq)
    x = x_ref[...]
    gffn = gffn_ref[...]
    lhs_ref[...] = jnp.concatenate(
        [_rms_rows(xprev_ref[...], gffn) * keep_prev,
         _rms_rows(x, gffn),
         _rms_rows(xnext_ref[...], gffn) * keep_next], axis=0).astype(BF16)
    u_bufs = ((ua0_ref, uv0_ref), (ua1_ref, uv1_ref))

    def cols(j, half):
        return pl.ds(half * D_FF + j * FF_CHUNK, FF_CHUNK)

    def up(j):
        lhs = lhs_ref[...]
        ua_ref, uv_ref = u_bufs[j % 2]
        ua_ref[...] = _dot(lhs, wup_ref[:, cols(j, 0)])
        uv_ref[...] = _dot(lhs, wup_ref[:, cols(j, 1)])

    def gate(j):
        ua_ref, uv_ref = u_bufs[j % 2]
        a = _conv3(ua_ref, wfc_ref[:, cols(j, 0)], tm)
        v = _conv3(uv_ref, wfc_ref[:, cols(j, 1)], tm)
        act_ref[:, cols(j, 0)] = (a * jax.nn.sigmoid(a) * v).astype(BF16)

    def down(first, last):
        span = pl.ds(first * FF_CHUNK, (last - first) * FF_CHUNK)
        return _dot(act_ref[:, span], wdn_ref[span, :])

    y = None
    up(0)
    for j in range(N_FF_CHUNKS):
        if j + 1 < N_FF_CHUNKS:
            up(j + 1)
        gate(j)
        if j + 1 in DOWN_SPLITS:
            first = ([0] + list(DOWN_SPLITS))[DOWN_SPLITS.index(j + 1)]
            part = down(first, j + 1)
            y = part if y is None else y + part

    x2 = x + y
    h3 = _rms_rows(x2, gple_ref[...]).astype(BF16)
    pg = jax.nn.sigmoid(_dot(h3, wpg_ref[...]))
    out_ref[...] = x2 + pg * out_ref[...]


def _ffn(layer, x2d, p3d, w, mixer_weights, seq):
    w_up_b, w_down_b, w_ple_gate_b, w_ple_b = mixer_weights
    n_tokens = x2d.shape[0]
    tm = FFN_TILE
    per_tile = tm // HALO
    last_blk = n_tokens // HALO - 1
    row = lambda width: pl.BlockSpec((tm, width), lambda i: (i, 0))
    return pl.pallas_call(
        functools.partial(_ffn_kernel, seq // tm),
        grid=(n_tokens // tm,),
        in_specs=[row(D_MODEL),
                  pl.BlockSpec((HALO, D_MODEL), lambda i: (jnp.maximum(i * per_tile - 1, 0), 0)),
                  pl.BlockSpec((HALO, D_MODEL),
                               lambda i: (jnp.minimum((i + 1) * per_tile, last_blk), 0)),
                  pl.BlockSpec((None, tm, PLE_DIM), lambda i: (layer, i, 0)),
                  _layer_spec(layer, 1, D_MODEL),
                  _whole_spec(D_MODEL, 2 * D_FF),
                  _layer_spec(layer, 3, 2 * D_FF),
                  _whole_spec(D_FF, D_MODEL),
                  _layer_spec(layer, 1, D_MODEL),
                  _whole_spec(D_MODEL, D_MODEL),
                  _whole_spec(PLE_DIM, D_MODEL)],
        out_specs=row(D_MODEL),
        out_shape=jax.ShapeDtypeStruct((n_tokens, D_MODEL), F32),
        scratch_shapes=[pltpu.VMEM((tm + 2 * HALO, D_MODEL), BF16)]
        + [pltpu.VMEM((tm + 2 * HALO, FF_CHUNK), F32)] * 4
        + [pltpu.VMEM((tm, D_FF), BF16)],
        compiler_params=_params(1),
        name="ffn",
    )(x2d, x2d, x2d, p3d, w["g_ffn"], w_up_b, w["w_ffn_conv"], w_down_b, w["g_ple"],
      w_ple_gate_b, w_ple_b)


def _prep_weights(w_in, b_gate, g_mix, g_q_lat, w_uq, g_kv_lat, w_ukv, g_q_head, g_k_head,
                  w_conv, g_ffn, w_ffn_conv, g_ple):
    depth = w_in.shape[0]
    as_row = lambda g: g.reshape(depth, 1, g.shape[-1])
    as_col = lambda g: g.reshape(depth, g.shape[-1], 1)

    uq = w_uq.reshape(depth, Q_LORA, N_HEADS, D_QK)
    uq = jnp.pad(uq, ((0, 0), (0, 0), (0, 0), (0, HEAD_PAD - D_QK)))
    w_uq_t = uq.reshape(depth, Q_LORA, N_HEADS * HEAD_PAD).transpose(0, 2, 1).astype(BF16)

    ukv = w_ukv.reshape(depth, KV_LORA, N_HEADS, D_NOPE + D_V)
    uk = ukv[..., :D_NOPE].reshape(depth, KV_LORA, N_HEADS * D_NOPE)
    uv = ukv[..., D_NOPE:].reshape(depth, KV_LORA, N_HEADS * D_V)
    w_ukv_t = jnp.concatenate([uk, uv], axis=2).transpose(0, 2, 1).astype(BF16)

    return {
        "g_mix": as_row(g_mix),
        "w_in": w_in.astype(BF16),
        "b_gate": as_row(b_gate),
        "g_q_lat": as_col(g_q_lat),
        "w_uq_t": w_uq_t,
        "g_kv_lat": as_col(g_kv_lat),
        "w_ukv_t": w_ukv_t,
        "g_q_head": as_col(g_q_head * (D_QK ** -0.5 * LOG2_E)),
        "g_k_head": as_col(g_k_head),
        "w_conv": w_conv,
        "g_ffn": as_row(g_ffn),
        "w_ffn_conv": w_ffn_conv,
        "g_ple": as_row(g_ple),
    }


def kernel(x, p, positions, w_in, b_gate, g_mix, g_q_lat, w_uq, g_kv_lat, w_ukv, g_q_head,
           g_k_head, w_attn_up, w_conv, w_conv_up, w_o, g_ffn, w_up, w_ffn_conv, w_down,
           g_ple, w_ple_gate, w_ple):
    batch, seq, d_model = x.shape
    depth = w_in.shape[0]
    assert d_model == D_MODEL and seq % ROW_TILE == 0 and seq % FFN_TILE == 0
    n_tokens = batch * seq
    tabs = _rope_tables(positions, n_tokens, seq)
    w = _prep_weights(w_in, b_gate, g_mix, g_q_lat, w_uq, g_kv_lat, w_ukv, g_q_head, g_k_head,
                      w_conv, g_ffn, w_ffn_conv, g_ple)
    x2d = x.reshape(n_tokens, D_MODEL)
    p3d = p.reshape(depth, n_tokens, PLE_DIM)
    for layer in range(depth):
        (qt, k, vt, cb, ccx), branch_weights = _inproj(
            layer, x2d, tabs, w, (w_attn_up, w_conv_up, w_o))
        yt, mixer_weights = _attention(
            layer, qt, k, vt, (w_up, w_down, w_ple_gate, w_ple), batch, seq)
        x2d = _merge(layer, x2d, yt, cb, ccx, w, branch_weights, seq)
        x2d = _ffn(layer, x2d, p3d, w, mixer_weights, seq)
    return x2d.reshape(batch, seq, D_MODEL)
```

```python
import functools

import numpy as np
import jax
import jax.numpy as jnp
from jax.experimental import pallas as pl
from jax.experimental.pallas import tpu as pltpu

D_MODEL = 1024
N_HEADS = 8
D_NOPE = 64
D_ROPE = 32
D_QK = D_NOPE + D_ROPE
D_V = 64
Q_LORA = 256
KV_LORA = 128
CONV_WIDTH = 512
D_FF = 2816
PLE_DIM = 256
EPS = 1e-6
ROPE_THETA = 10000.0

HEAD_PAD = 128
HALF_ROPE = D_ROPE // 2
V_ROWS = D_V + 16
N_LATENT = Q_LORA + KV_LORA + D_ROPE
LATENT_COLS = 512
N_CONV_IN = 3 * CONV_WIDTH
N_GATES = 2 * D_MODEL
W_IN_WINDOW = 2048
assert N_LATENT + N_CONV_IN <= W_IN_WINDOW
WEIGHT_PREP_ROWS = 128
HALO = 8
BF16_SUBLANES = 16
FF_CHUNK = 256
N_FF_CHUNKS = D_FF // FF_CHUNK
DOWN_SPLITS = (N_FF_CHUNKS,)
VMEM_LIMIT_BYTES = 56 * 1024 * 1024

ROW_TILE = 1024
Q_TILE = 512
Q_PER_ROW_TILE = ROW_TILE // Q_TILE
FFN_TILE = 1024
KEY_CHUNK = 256
HEADS_PER_STEP = 8
LOG2_E = 1.4426950408889634

BF16 = jnp.bfloat16
F32 = jnp.float32


def _dot(a, b):
    return jnp.dot(a, b, preferred_element_type=F32)


def _dot_nt(a, b):
    return jax.lax.dot_general(a, b, (((1,), (1,)), ((), ())), preferred_element_type=F32)


def _rms_rows(v, gain):
    return v * jax.lax.rsqrt(jnp.mean(v * v, axis=-1, keepdims=True) + EPS) * gain


def _rms_cols(v, gain):
    return v * jax.lax.rsqrt(jnp.mean(v * v, axis=0, keepdims=True) + EPS) * gain


def _rope_cols(t, cos, sin):
    t1, t2 = t[:HALF_ROPE], t[HALF_ROPE:]
    return t1 * cos - t2 * sin, t2 * cos + t1 * sin


def _cut_columns(dst_ref, src_ref, first_col):
    n_rows, width = dst_ref.shape

    def band(r, carry):
        rows = pl.ds(pl.multiple_of(r * WEIGHT_PREP_ROWS, WEIGHT_PREP_ROWS), WEIGHT_PREP_ROWS)
        words = pltpu.bitcast(src_ref[rows, :], jnp.uint32)
        dst_ref[rows, :] = pltpu.bitcast(words[:, first_col:first_col + width], BF16)
        return carry

    jax.lax.fori_loop(0, n_rows // WEIGHT_PREP_ROWS, band, 0)


def _cast_plan(layer, stacks, n_steps, step_of):
    in_specs, out_specs, out_shapes = [], [], []
    for stack in stacks:
        _, n_rows, n_cols = stack.shape
        slab = n_rows // n_steps
        assert slab * n_steps == n_rows and slab % BF16_SUBLANES == 0
        in_specs.append(pl.BlockSpec((None, slab, n_cols),
                                     lambda *idx: (layer, step_of(*idx), 0)))
        out_specs.append(pl.BlockSpec((slab, n_cols), lambda *idx: (step_of(*idx), 0)))
        out_shapes.append(jax.ShapeDtypeStruct((n_rows, n_cols), BF16))
    return in_specs, out_specs, out_shapes


def _cast_slabs(src_refs, dst_refs):
    for src_ref, dst_ref in zip(src_refs, dst_refs):
        dst_ref[...] = src_ref[...].astype(BF16)


def _whole_spec(rows, cols):
    return pl.BlockSpec((rows, cols), lambda *_: (0, 0), pipeline_mode=pl.Buffered(1))


def _layer_spec(layer, rows, cols):
    return pl.BlockSpec((None, rows, cols), lambda *_: (layer, 0, 0),
                        pipeline_mode=pl.Buffered(1))


def _params(n_axes):
    return pltpu.CompilerParams(
        dimension_semantics=("arbitrary",) * n_axes,
        vmem_limit_bytes=VMEM_LIMIT_BYTES)


def _rope_table_kernel(pos_ref, inv_ref, cos_ref, sin_ref):
    ang = pos_ref[...].astype(F32) * inv_ref[...]
    cos_ref[...] = jnp.cos(ang)
    sin_ref[...] = jnp.sin(ang)


def _rope_tables(positions, n_tokens, seq):
    inv = ROPE_THETA ** (-np.arange(0, D_ROPE, 2, dtype=np.float32) / D_ROPE)
    inv = jnp.asarray(inv.astype(np.float32)).reshape(HALF_ROPE, 1)
    pos = positions.reshape(1, n_tokens)
    out = jax.ShapeDtypeStruct((HALF_ROPE, n_tokens), F32)
    return pl.pallas_call(
        _rope_table_kernel,
        grid=(n_tokens // seq,),
        in_specs=[pl.BlockSpec((1, seq), lambda i: (0, i)),
                  pl.BlockSpec((HALF_ROPE, 1), lambda i: (0, 0))],
        out_specs=[pl.BlockSpec((HALF_ROPE, seq), lambda i: (0, i))] * 2,
        out_shape=[out, out],
        compiler_params=_params(1),
        name="rope_tables",
    )(pos, inv)


def _inproj_kernel(n_cast, x_ref, gmix_ref, win_ref, gq_ref, wuq_ref, gkv_ref,
                   wukv_ref, gqh_ref, gkh_ref, cos_ref, sin_ref, *refs):
    cast_src, outs = refs[:n_cast], refs[n_cast:]
    qt_ref, k_ref, vt_ref, cb_ref, ccx_ref = outs[:5]
    cast_dst = outs[5:]
    _cast_slabs(cast_src, cast_dst)

    h = _rms_rows(x_ref[...], gmix_ref[...]).astype(BF16)

    conv_lo = N_LATENT
    zt = _dot_nt(win_ref[:N_LATENT], h)
    cb_ref[...] = _dot_nt(h, win_ref[conv_lo:conv_lo + CONV_WIDTH]).astype(BF16)

    cqn = _rms_cols(zt[:Q_LORA], gq_ref[...]).astype(BF16)
    qt = _dot(wuq_ref[...], cqn)
    ckvn = _rms_cols(zt[Q_LORA:Q_LORA + KV_LORA], gkv_ref[...]).astype(BF16)
    kvt = _dot(wukv_ref[...], ckvn)

    zcx = _dot_nt(h, win_ref[conv_lo + CONV_WIDTH:])
    ccx_ref[...] = (zcx[:, :CONV_WIDTH] * zcx[:, CONV_WIDTH:]).astype(BF16)

    cos = cos_ref[...]
    sin = sin_ref[...]
    gqh = gqh_ref[...]
    for hd in range(N_HEADS):
        blk = qt[hd * HEAD_PAD:(hd + 1) * HEAD_PAD]
        qn = _rms_cols(blk[:D_NOPE], gqh[:D_NOPE])
        qr = _rms_cols(blk[D_NOPE:D_QK], gqh[D_NOPE:])
        r1, r2 = _rope_cols(qr, cos, sin)
        full = jnp.concatenate([qn, r1, r2, blk[D_QK:]], axis=0)
        full = full.astype(BF16)
        for j in range(Q_PER_ROW_TILE):
            qt_ref[j, hd] = full[:, j * Q_TILE:(j + 1) * Q_TILE]

    for hd in range(N_HEADS):
        lo = N_HEADS * D_NOPE + hd * D_V
        vt_ref[hd, 0:D_V, :] = kvt[lo:lo + D_V].astype(BF16)
        vt_ref[hd, D_V:, :] = jnp.ones((V_ROWS - D_V, kvt.shape[1]), BF16)

    gkh = gkh_ref[...]
    kr = _rms_cols(zt[Q_LORA + KV_LORA:N_LATENT], gkh[D_NOPE:])
    kr1, kr2 = _rope_cols(kr, cos, sin)
    pad = jnp.zeros((HEAD_PAD - D_QK, kr.shape[1]), F32)
    for hd in range(N_HEADS):
        kn = _rms_cols(kvt[hd * D_NOPE:(hd + 1) * D_NOPE], gkh[:D_NOPE])
        full = jnp.concatenate([kn, kr1, kr2, pad], axis=0)
        k_ref[hd] = full.T.astype(BF16)


def _inproj(layer, x2d, tabs, w, cast_stacks):
    n_tokens = x2d.shape[0]
    tm = ROW_TILE
    cos_t, sin_t = tabs
    cast_in, cast_out, cast_shapes = _cast_plan(layer, cast_stacks, n_tokens // tm, lambda i: i)
    row = lambda width: pl.BlockSpec((tm, width), lambda i: (i, 0))
    col = lambda height: pl.BlockSpec((height, tm), lambda i: (0, i))
    outs = [
        jax.ShapeDtypeStruct((n_tokens // Q_TILE, N_HEADS, HEAD_PAD, Q_TILE), BF16),
        jax.ShapeDtypeStruct((N_HEADS, n_tokens, HEAD_PAD), BF16),
        jax.ShapeDtypeStruct((N_HEADS, V_ROWS, n_tokens), BF16),
        jax.ShapeDtypeStruct((n_tokens, CONV_WIDTH), BF16),
        jax.ShapeDtypeStruct((n_tokens, CONV_WIDTH), BF16),
    ]
    results = pl.pallas_call(
        functools.partial(_inproj_kernel, len(cast_stacks)),
        grid=(n_tokens // tm,),
        in_specs=[row(D_MODEL),
                  _layer_spec(layer, 1, D_MODEL),
                  _layer_spec(layer, N_LATENT + N_CONV_IN, D_MODEL),
                  _layer_spec(layer, Q_LORA, 1),
                  _layer_spec(layer, N_HEADS * HEAD_PAD, Q_LORA),
                  _layer_spec(layer, KV_LORA, 1),
                  _layer_spec(layer, N_HEADS * (D_NOPE + D_V), KV_LORA),
                  _layer_spec(layer, D_QK, 1),
                  _layer_spec(layer, D_QK, 1),
                  col(HALF_ROPE), col(HALF_ROPE)] + cast_in,
        out_specs=[pl.BlockSpec((Q_PER_ROW_TILE, N_HEADS, HEAD_PAD, Q_TILE), lambda i: (i, 0, 0, 0)),
                   pl.BlockSpec((N_HEADS, tm, HEAD_PAD), lambda i: (0, i, 0)),
                   pl.BlockSpec((N_HEADS, V_ROWS, tm), lambda i: (0, 0, i)),
                   row(CONV_WIDTH), row(CONV_WIDTH)] + cast_out,
        out_shape=outs + cast_shapes,
        compiler_params=_params(1),
        name="inproj",
    )(x2d, w["g_mix"], w["w_in"], w["g_q_lat"], w["w_uq_t"], w["g_kv_lat"],
      w["w_ukv_t"], w["g_q_head"], w["g_k_head"], cos_t, sin_t, *cast_stacks)
    return results[:5], results[5:]


def _attention_kernel(n_cast, qt_ref, k_ref, vt_ref, *refs):
    cast_src, (yt_ref, *cast_dst) = refs[:n_cast], refs[n_cast:2 * n_cast + 1]
    s0_ref, s1_ref = refs[2 * n_cast + 1:]
    _cast_slabs(cast_src, cast_dst)
    nq, heads = qt_ref.shape[:2]
    s_refs = (s0_ref, s1_ref)

    seq = k_ref.shape[1]
    n_chunks = seq // KEY_CHUNK
    rows = lambda c: pl.ds(c * KEY_CHUNK, KEY_CHUNK)

    def score_chunk(q, hd, c, m):
        s = _dot(k_ref[hd, rows(c), :], q)
        s_refs[hd % 2][rows(c), :] = s
        part = jnp.max(s, axis=0, keepdims=True)
        return part if m is None else jnp.maximum(m, part)

    def value_chunk(hd, c, m, o):
        p = jnp.exp2(s_refs[hd % 2][rows(c), :] - m).astype(BF16)
        part = _dot(vt_ref[hd, :, rows(c)], p)
        return part if o is None else o + part

    def emit(qi, hd, o):
        l = o[D_V:D_V + 1]
        yt_ref[qi, hd] = (o[:D_V] * (1.0 / l)).astype(BF16)

    def scores(qi, hd):
        q = qt_ref[qi, hd]
        m = None
        for c in range(n_chunks):
            m = score_chunk(q, hd, c, m)
        return m

    def overlapped(qi_scores, hd_scores, qi_values, hd_values, m_values):
        q = qt_ref[qi_scores, hd_scores]
        m, o = None, None
        for c in range(n_chunks):
            m = score_chunk(q, hd_scores, c, m)
            o = value_chunk(hd_values, c, m_values, o)
        emit(qi_values, hd_values, o)
        return m

    def step(qi, m):
        for hd in range(heads - 1):
            m = overlapped(qi, hd + 1, qi, hd, m)
        return overlapped(qi + 1, 0, qi, heads - 1, m)

    m = jax.lax.fori_loop(0, nq - 1, step, scores(0, 0))
    for hd in range(heads - 1):
        m = overlapped(nq - 1, hd + 1, nq - 1, hd, m)
    o = None
    for c in range(n_chunks):
        o = value_chunk(heads - 1, c, m, o)
    emit(nq - 1, heads - 1, o)


def _attention(layer, qt, k, vt, cast_stacks, batch, seq):
    tq = Q_TILE
    nq = seq // tq
    n_groups = N_HEADS // HEADS_PER_STEP
    cast_in, cast_out, cast_shapes = _cast_plan(
        layer, cast_stacks, batch * n_groups, lambda b, g: b * n_groups + g)
    outs = pl.pallas_call(
        functools.partial(_attention_kernel, len(cast_stacks)),
        grid=(batch, n_groups),
        in_specs=[pl.BlockSpec((nq, HEADS_PER_STEP, HEAD_PAD, tq), lambda b, g: (b, g, 0, 0)),
                  pl.BlockSpec((HEADS_PER_STEP, seq, HEAD_PAD), lambda b, g: (g, b, 0)),
                  pl.BlockSpec((HEADS_PER_STEP, V_ROWS, seq), lambda b, g: (g, 0, b))] + cast_in,
        out_specs=[pl.BlockSpec((nq, HEADS_PER_STEP, D_V, tq), lambda b, g: (b, g, 0, 0))]
        + cast_out,
        out_shape=[jax.ShapeDtypeStruct((batch * nq, N_HEADS, D_V, tq), BF16)] + cast_shapes,
        scratch_shapes=[pltpu.VMEM((seq, tq), F32)] * 2,
        compiler_params=_params(2),
        name="attention",
    )(qt, k, vt, *cast_stacks)
    return outs[0], outs[1:]


def _conv3(buf_ref, w, n_rows):
    u = buf_ref[...]
    total = u.shape[0]
    full = (pltpu.roll(w[0:1] * u, 1, axis=0)
            + w[1:2] * u
            + pltpu.roll(w[2:3] * u, total - 1, axis=0))
    return full[HALO:HALO + n_rows]


def _edge_masks(tiles_per_seq):
    j = pl.program_id(0) % tiles_per_seq
    return (j != 0).astype(F32), (j != tiles_per_seq - 1).astype(F32)


def _merge_kernel(tiles_per_seq, x_ref, yt_ref, cb_ref, ccx_ref, ccx_prev_ref, ccx_next_ref,
                  gmix_ref, win_ref, bgate_ref, wconv_ref, wau_ref, wcu_ref, wo_ref,
                  out_ref, cs_ref):
    tm = x_ref.shape[0]
    keep_prev, keep_next = _edge_masks(tiles_per_seq)
    x = x_ref[...]
    h = _rms_rows(x, gmix_ref[...]).astype(BF16)

    two = 2 * HALO
    cs_ref[0:HALO, :] = ccx_prev_ref[...].astype(F32)[two - HALO:] * keep_prev
    cs_ref[HALO:HALO + tm, :] = ccx_ref[...].astype(F32)
    cs_ref[HALO + tm:, :] = ccx_next_ref[...].astype(F32)[:HALO] * keep_next
    y_conv = (cb_ref[...].astype(F32) * _conv3(cs_ref, wconv_ref[...], tm)).astype(BF16)

    y_attn = jnp.concatenate(
        [yt_ref[j].reshape(N_HEADS * D_V, Q_TILE).astype(F32).T for j in range(Q_PER_ROW_TILE)],
        axis=0).astype(BF16)
    gates = jax.nn.sigmoid(_dot_nt(h, win_ref[N_LATENT + N_CONV_IN:]) + bgate_ref[...])
    merged = (gates[:, :D_MODEL] * _dot(y_attn, wau_ref[...])
              + gates[:, D_MODEL:] * _dot(y_conv, wcu_ref[...]))
    out_ref[...] = x + _dot(merged.astype(BF16), wo_ref[...])


def _merge(layer, x2d, yt, cb, ccx, w, branch_weights, seq):
    n_tokens = x2d.shape[0]
    tm = ROW_TILE
    two = 2 * HALO
    per_tile = tm // two
    last_blk = n_tokens // two - 1
    row = lambda width: pl.BlockSpec((tm, width), lambda i: (i, 0))
    return pl.pallas_call(
        functools.partial(_merge_kernel, seq // tm),
        grid=(n_tokens // tm,),
        in_specs=[row(D_MODEL),
                  pl.BlockSpec((Q_PER_ROW_TILE, N_HEADS, D_V, Q_TILE), lambda i: (i, 0, 0, 0)),
                  row(CONV_WIDTH), row(CONV_WIDTH),
                  pl.BlockSpec((two, CONV_WIDTH), lambda i: (jnp.maximum(i * per_tile - 1, 0), 0)),
                  pl.BlockSpec((two, CONV_WIDTH),
                               lambda i: (jnp.minimum((i + 1) * per_tile, last_blk), 0)),
                  _layer_spec(layer, 1, D_MODEL),
                  _layer_spec(layer, N_LATENT + N_CONV_IN + N_GATES, D_MODEL),
                  _layer_spec(layer, 1, N_GATES),
                  _layer_spec(layer, 3, CONV_WIDTH),
                  _whole_spec(N_HEADS * D_V, D_MODEL),
                  _whole_spec(CONV_WIDTH, D_MODEL),
                  _whole_spec(D_MODEL, D_MODEL)],
        out_specs=row(D_MODEL),
        out_shape=jax.ShapeDtypeStruct((n_tokens, D_MODEL), F32),
        scratch_shapes=[pltpu.VMEM((tm + 2 * HALO, CONV_WIDTH), F32)],
        compiler_params=_params(1),
        name="merge",
    )(x2d, yt, cb, ccx, ccx, ccx, w["g_mix"], w["w_in"], w["b_gate"], w["w_conv"],
      *branch_weights)


def _ffn_kernel(tiles_per_seq, x_ref, xprev_ref, xnext_ref, p_ref, gffn_ref, wup_ref, wfc_ref,
                wdn_ref, gple_ref, wpg_ref, wple_ref, out_ref, lhs_ref, ua0_ref, uv0_ref,
                ua1_ref, uv1_ref, act_ref):
    tm = x_ref.shape[0]
    out_ref[...] = _dot(p_ref[...].astype(BF16), wple_ref[...])
    keep_prev, keep_next = _edge_masks(tiles_per_seq)
    x = x_ref[...]
    gffn = gffn_ref[...]
    lhs_ref[...] = jnp.concatenate(
        [_rms_rows(xprev_ref[...], gffn) * keep_prev,
         _rms_rows(x, gffn),
         _rms_rows(xnext_ref[...], gffn) * keep_next], axis=0).astype(BF16)
    u_bufs = ((ua0_ref, uv0_ref), (ua1_ref, uv1_ref))

    def cols(j, half):
        return pl.ds(half * D_FF + j * FF_CHUNK, FF_CHUNK)

    def up(j):
        lhs = lhs_ref[...]
        ua_ref, uv_ref = u_bufs[j % 2]
        ua_ref[...] = _dot(lhs, wup_ref[:, cols(j, 0)])
        uv_ref[...] = _dot(lhs, wup_ref[:, cols(j, 1)])

    def gate(j):
        ua_ref, uv_ref = u_bufs[j % 2]
        a = _conv3(ua_ref, wfc_ref[:, cols(j, 0)], tm)
        v = _conv3(uv_ref, wfc_ref[:, cols(j, 1)], tm)
        act_ref[:, cols(j, 0)] = (a * jax.nn.sigmoid(a) * v).astype(BF16)

    def down(first, last):
        span = pl.ds(first * FF_CHUNK, (last - first) * FF_CHUNK)
        return _dot(act_ref[:, span], wdn_ref[span, :])

    y = None
    up(0)
    for j in range(N_FF_CHUNKS):
        if j + 1 < N_FF_CHUNKS:
            up(j + 1)
        gate(j)
        if j + 1 in DOWN_SPLITS:
            first = ([0] + list(DOWN_SPLITS))[DOWN_SPLITS.index(j + 1)]
            part = down(first, j + 1)
            y = part if y is None else y + part

    x2 = x + y
    h3 = _rms_rows(x2, gple_ref[...]).astype(BF16)
    pg = jax.nn.sigmoid(_dot(h3, wpg_ref[...]))
    out_ref[...] = x2 + pg * out_ref[...]


def _ffn(layer, x2d, p3d, w, mixer_weights, seq):
    w_up_b, w_down_b, w_ple_gate_b, w_ple_b = mixer_weights
    n_tokens = x2d.shape[0]
    tm = FFN_TILE
    per_tile = tm // HALO
    last_blk = n_tokens // HALO - 1
    row = lambda width: pl.BlockSpec((tm, width), lambda i: (i, 0))
    return pl.pallas_call(
        functools.partial(_ffn_kernel, seq // tm),
        grid=(n_tokens // tm,),
        in_specs=[row(D_MODEL),
                  pl.BlockSpec((HALO, D_MODEL), lambda i: (jnp.maximum(i * per_tile - 1, 0), 0)),
                  pl.BlockSpec((HALO, D_MODEL),
                               lambda i: (jnp.minimum((i + 1) * per_tile, last_blk), 0)),
                  pl.BlockSpec((None, tm, PLE_DIM), lambda i: (layer, i, 0)),
                  _layer_spec(layer, 1, D_MODEL),
                  _whole_spec(D_MODEL, 2 * D_FF),
                  _layer_spec(layer, 3, 2 * D_FF),
                  _whole_spec(D_FF, D_MODEL),
                  _layer_spec(layer, 1, D_MODEL),
                  _whole_spec(D_MODEL, D_MODEL),
                  _whole_spec(PLE_DIM, D_MODEL)],
        out_specs=row(D_MODEL),
        out_shape=jax.ShapeDtypeStruct((n_tokens, D_MODEL), F32),
        scratch_shapes=[pltpu.VMEM((tm + 2 * HALO, D_MODEL), BF16)]
        + [pltpu.VMEM((tm + 2 * HALO, FF_CHUNK), F32)] * 4
        + [pltpu.VMEM((tm, D_FF), BF16)],
        compiler_params=_params(1),
        name="ffn",
    )(x2d, x2d, x2d, p3d, w["g_ffn"], w_up_b, w["w_ffn_conv"], w_down_b, w["g_ple"],
      w_ple_gate_b, w_ple_b)


def _prep_weights(w_in, b_gate, g_mix, g_q_lat, w_uq, g_kv_lat, w_ukv, g_q_head, g_k_head,
                  w_conv, g_ffn, w_ffn_conv, g_ple):
    depth = w_in.shape[0]
    as_row = lambda g: g.reshape(depth, 1, g.shape[-1])
    as_col = lambda g: g.reshape(depth, g.shape[-1], 1)

    uq = w_uq.reshape(depth, Q_LORA, N_HEADS, D_QK)
    uq = jnp.pad(uq, ((0, 0), (0, 0), (0, 0), (0, HEAD_PAD - D_QK)))
    w_uq_t = uq.reshape(depth, Q_LORA, N_HEADS * HEAD_PAD).transpose(0, 2, 1).astype(BF16)

    ukv = w_ukv.reshape(depth, KV_LORA, N_HEADS, D_NOPE + D_V)
    uk = ukv[..., :D_NOPE].reshape(depth, KV_LORA, N_HEADS * D_NOPE)
    uv = ukv[..., D_NOPE:].reshape(depth, KV_LORA, N_HEADS * D_V)
    w_ukv_t = jnp.concatenate([uk, uv], axis=2).transpose(0, 2, 1).astype(BF16)

    return {
        "g_mix": as_row(g_mix),
        "w_in": w_in.transpose(0, 2, 1).astype(BF16),
        "b_gate": as_row(b_gate),
        "g_q_lat": as_col(g_q_lat),
        "w_uq_t": w_uq_t,
        "g_kv_lat": as_col(g_kv_lat),
        "w_ukv_t": w_ukv_t,
        "g_q_head": as_col(g_q_head * (D_QK ** -0.5 * LOG2_E)),
        "g_k_head": as_col(g_k_head),
        "w_conv": w_conv,
        "g_ffn": as_row(g_ffn),
        "w_ffn_conv": w_ffn_conv,
        "g_ple": as_row(g_ple),
    }


def kernel(x, p, positions, w_in, b_gate, g_mix, g_q_lat, w_uq, g_kv_lat, w_ukv, g_q_head,
           g_k_head, w_attn_up, w_conv, w_conv_up, w_o, g_ffn, w_up, w_ffn_conv, w_down,
           g_ple, w_ple_gate, w_ple):
    batch, seq, d_model = x.shape
    depth = w_in.shape[0]
    assert d_model == D_MODEL and seq % ROW_TILE == 0 and seq % FFN_TILE == 0
    n_tokens = batch * seq
    tabs = _rope_tables(positions, n_tokens, seq)
    w = _prep_weights(w_in, b_gate, g_mix, g_q_lat, w_uq, g_kv_lat, w_ukv, g_q_head, g_k_head,
                      w_conv, g_ffn, w_ffn_conv, g_ple)
    x2d = x.reshape(n_tokens, D_MODEL)
    p3d = p.reshape(depth, n_tokens, PLE_DIM)
    for layer in range(depth):
        (qt, k, vt, cb, ccx), branch_weights = _inproj(
            layer, x2d, tabs, w, (w_attn_up, w_conv_up, w_o))
        yt, mixer_weights = _attention(
            layer, qt, k, vt, (w_up, w_down, w_ple_gate, w_ple), batch, seq)
        x2d = _merge(layer, x2d, yt, cb, ccx, w, branch_weights, seq)
        x2d = _ffn(layer, x2d, p3d, w, mixer_weights, seq)
    return x2d.reshape(batch, seq, D_MODEL)
```

```python
import functools

import numpy as np
import jax
import jax.numpy as jnp
from jax.experimental import pallas as pl
from jax.experimental.pallas import tpu as pltpu

D_MODEL = 1024
N_HEADS = 8
D_NOPE = 64
D_ROPE = 32
D_QK = D_NOPE + D_ROPE
D_V = 64
Q_LORA = 256
KV_LORA = 128
CONV_WIDTH = 512
D_FF = 2816
PLE_DIM = 256
EPS = 1e-6
ROPE_THETA = 10000.0

HEAD_PAD = 128
HALF_ROPE = D_ROPE // 2
V_ROWS = D_V + 16
N_LATENT = Q_LORA + KV_LORA + D_ROPE
N_CONV_IN = 3 * CONV_WIDTH
N_GATES = 2 * D_MODEL
HALO = 8
BF16_SUBLANES = 16
FF_CHUNK = 256
N_FF_CHUNKS = D_FF // FF_CHUNK
DOWN_SPLITS = (N_FF_CHUNKS,)
VMEM_LIMIT_BYTES = 56 * 1024 * 1024

ROW_TILE = 1024
Q_TILE = 512
Q_PER_ROW_TILE = ROW_TILE // Q_TILE
FFN_TILE = 1024
KEY_CHUNK = 256
HEADS_PER_STEP = 8
LOG2_E = 1.4426950408889634

BF16 = jnp.bfloat16
F32 = jnp.float32


def _dot(a, b):
    return jnp.dot(a, b, preferred_element_type=F32)


def _dot_nt(a, b):
    return jax.lax.dot_general(a, b, (((1,), (1,)), ((), ())), preferred_element_type=F32)


def _rms_rows(v, gain):
    return v * jax.lax.rsqrt(jnp.mean(v * v, axis=-1, keepdims=True) + EPS) * gain


def _rms_cols(v, gain):
    return v * jax.lax.rsqrt(jnp.mean(v * v, axis=0, keepdims=True) + EPS) * gain


def _rope_cols(t, cos, sin):
    t1, t2 = t[:HALF_ROPE], t[HALF_ROPE:]
    return t1 * cos - t2 * sin, t2 * cos + t1 * sin


def _cast_plan(layer, stacks, n_steps, step_of):
    in_specs, out_specs, out_shapes = [], [], []
    for stack in stacks:
        _, n_rows, n_cols = stack.shape
        slab = n_rows // n_steps
        assert slab * n_steps == n_rows and slab % BF16_SUBLANES == 0
        in_specs.append(pl.BlockSpec((None, slab, n_cols),
                                     lambda *idx: (layer, step_of(*idx), 0)))
        out_specs.append(pl.BlockSpec((slab, n_cols), lambda *idx: (step_of(*idx), 0)))
        out_shapes.append(jax.ShapeDtypeStruct((n_rows, n_cols), BF16))
    return in_specs, out_specs, out_shapes


def _cast_slabs(src_refs, dst_refs):
    for src_ref, dst_ref in zip(src_refs, dst_refs):
        dst_ref[...] = src_ref[...].astype(BF16)


def _whole_spec(rows, cols):
    return pl.BlockSpec((rows, cols), lambda *_: (0, 0), pipeline_mode=pl.Buffered(1))


def _layer_spec(layer, rows, cols):
    return pl.BlockSpec((None, rows, cols), lambda *_: (layer, 0, 0),
                        pipeline_mode=pl.Buffered(1))


def _params(n_axes):
    return pltpu.CompilerParams(
        dimension_semantics=("arbitrary",) * n_axes,
        vmem_limit_bytes=VMEM_LIMIT_BYTES)


def _rope_table_kernel(pos_ref, inv_ref, cos_ref, sin_ref):
    ang = pos_ref[...].astype(F32) * inv_ref[...]
    cos_ref[...] = jnp.cos(ang)
    sin_ref[...] = jnp.sin(ang)


def _rope_tables(positions, n_tokens, seq):
    inv = ROPE_THETA ** (-np.arange(0, D_ROPE, 2, dtype=np.float32) / D_ROPE)
    inv = jnp.asarray(inv.astype(np.float32)).reshape(HALF_ROPE, 1)
    pos = positions.reshape(1, n_tokens)
    out = jax.ShapeDtypeStruct((HALF_ROPE, n_tokens), F32)
    return pl.pallas_call(
        _rope_table_kernel,
        grid=(n_tokens // seq,),
        in_specs=[pl.BlockSpec((1, seq), lambda i: (0, i)),
                  pl.BlockSpec((HALF_ROPE, 1), lambda i: (0, 0))],
        out_specs=[pl.BlockSpec((HALF_ROPE, seq), lambda i: (0, i))] * 2,
        out_shape=[out, out],
        compiler_params=_params(1),
        name="rope_tables",
    )(pos, inv)


def _inproj_kernel(n_cast, x_ref, gmix_ref, win_ref, gq_ref, wuq_ref, gkv_ref,
                   wukv_ref, gqh_ref, gkh_ref, cos_ref, sin_ref, *refs):
    cast_src, outs = refs[:n_cast], refs[n_cast:]
    qt_ref, k_ref, vt_ref, cb_ref, ccx_ref = outs[:5]
    cast_dst = outs[5:]
    _cast_slabs(cast_src, cast_dst)

    h = _rms_rows(x_ref[...], gmix_ref[...]).astype(BF16)

    conv_lo = N_LATENT
    zt = _dot_nt(win_ref[:N_LATENT], h)
    cb_ref[...] = _dot_nt(h, win_ref[conv_lo:conv_lo + CONV_WIDTH]).astype(BF16)

    cqn = _rms_cols(zt[:Q_LORA], gq_ref[...]).astype(BF16)
    qt = _dot(wuq_ref[...], cqn)
    ckvn = _rms_cols(zt[Q_LORA:Q_LORA + KV_LORA], gkv_ref[...]).astype(BF16)
    kvt = _dot(wukv_ref[...], ckvn)

    zcx = _dot_nt(h, win_ref[conv_lo + CONV_WIDTH:])
    ccx_ref[...] = (zcx[:, :CONV_WIDTH] * zcx[:, CONV_WIDTH:]).astype(BF16)

    cos = cos_ref[...]
    sin = sin_ref[...]
    gqh = gqh_ref[...]
    for hd in range(N_HEADS):
        blk = qt[hd * HEAD_PAD:(hd + 1) * HEAD_PAD]
        qn = _rms_cols(blk[:D_NOPE], gqh[:D_NOPE])
        qr = _rms_cols(blk[D_NOPE:D_QK], gqh[D_NOPE:])
        r1, r2 = _rope_cols(qr, cos, sin)
        full = jnp.concatenate([qn, r1, r2, blk[D_QK:]], axis=0)
        full = full.astype(BF16)
        for j in range(Q_PER_ROW_TILE):
            qt_ref[j, hd] = full[:, j * Q_TILE:(j + 1) * Q_TILE]

    for hd in range(N_HEADS):
        lo = N_HEADS * D_NOPE + hd * D_V
        vt_ref[hd, 0:D_V, :] = kvt[lo:lo + D_V].astype(BF16)
        vt_ref[hd, D_V:, :] = jnp.ones((V_ROWS - D_V, kvt.shape[1]), BF16)

    gkh = gkh_ref[...]
    kr = _rms_cols(zt[Q_LORA + KV_LORA:N_LATENT], gkh[D_NOPE:])
    kr1, kr2 = _rope_cols(kr, cos, sin)
    pad = jnp.zeros((HEAD_PAD - D_QK, kr.shape[1]), F32)
    for hd in range(N_HEADS):
        kn = _rms_cols(kvt[hd * D_NOPE:(hd + 1) * D_NOPE], gkh[:D_NOPE])
        full = jnp.concatenate([kn, kr1, kr2, pad], axis=0)
        k_ref[hd] = full.T.astype(BF16)


def _inproj(layer, x2d, tabs, w, cast_stacks):
    n_tokens = x2d.shape[0]
    tm = ROW_TILE
    cos_t, sin_t = tabs
    cast_in, cast_out, cast_shapes = _cast_plan(layer, cast_stacks, n_tokens // tm, lambda i: i)
    row = lambda width: pl.BlockSpec((tm, width), lambda i: (i, 0))
    col = lambda height: pl.BlockSpec((height, tm), lambda i: (0, i))
    outs = [
        jax.ShapeDtypeStruct((n_tokens // Q_TILE, N_HEADS, HEAD_PAD, Q_TILE), BF16),
        jax.ShapeDtypeStruct((N_HEADS, n_tokens, HEAD_PAD), BF16),
        jax.ShapeDtypeStruct((N_HEADS, V_ROWS, n_tokens), BF16),
        jax.ShapeDtypeStruct((n_tokens, CONV_WIDTH), BF16),
        jax.ShapeDtypeStruct((n_tokens, CONV_WIDTH), BF16),
    ]
    results = pl.pallas_call(
        functools.partial(_inproj_kernel, len(cast_stacks)),
        grid=(n_tokens // tm,),
        in_specs=[row(D_MODEL),
                  _layer_spec(layer, 1, D_MODEL),
                  _layer_spec(layer, N_LATENT + N_CONV_IN, D_MODEL),
                  _layer_spec(layer, Q_LORA, 1),
                  _layer_spec(layer, N_HEADS * HEAD_PAD, Q_LORA),
                  _layer_spec(layer, KV_LORA, 1),
                  _layer_spec(layer, N_HEADS * (D_NOPE + D_V), KV_LORA),
                  _layer_spec(layer, D_QK, 1),
                  _layer_spec(layer, D_QK, 1),
                  col(HALF_ROPE), col(HALF_ROPE)] + cast_in,
        out_specs=[pl.BlockSpec((Q_PER_ROW_TILE, N_HEADS, HEAD_PAD, Q_TILE), lambda i: (i, 0, 0, 0)),
                   pl.BlockSpec((N_HEADS, tm, HEAD_PAD), lambda i: (0, i, 0)),
                   pl.BlockSpec((N_HEADS, V_ROWS, tm), lambda i: (0, 0, i)),
                   row(CONV_WIDTH), row(CONV_WIDTH)] + cast_out,
        out_shape=outs + cast_shapes,
        compiler_params=_params(1),
        name="inproj",
    )(x2d, w["g_mix"], w["w_in"], w["g_q_lat"], w["w_uq_t"], w["g_kv_lat"],
      w["w_ukv_t"], w["g_q_head"], w["g_k_head"], cos_t, sin_t, *cast_stacks)
    return results[:5], results[5:]


def _attention_kernel(n_cast, qt_ref, k_ref, vt_ref, *refs):
    cast_src, (yt_ref, *cast_dst) = refs[:n_cast], refs[n_cast:2 * n_cast + 1]
    s0_ref, s1_ref = refs[2 * n_cast + 1:]
    _cast_slabs(cast_src, cast_dst)
    nq, heads = qt_ref.shape[:2]
    s_refs = (s0_ref, s1_ref)

    seq = k_ref.shape[1]
    n_chunks = seq // KEY_CHUNK
    rows = lambda c: pl.ds(c * KEY_CHUNK, KEY_CHUNK)

    def score_chunk(q, hd, c, m):
        s = _dot(k_ref[hd, rows(c), :], q)
        s_refs[hd % 2][rows(c), :] = s
        part = jnp.max(s, axis=0, keepdims=True)
        return part if m is None else jnp.maximum(m, part)

    def value_chunk(hd, c, m, o):
        p = jnp.exp2(s_refs[hd % 2][rows(c), :] - m).astype(BF16)
        part = _dot(vt_ref[hd, :, rows(c)], p)
        return part if o is None else o + part

    def emit(qi, hd, o):
        l = o[D_V:D_V + 1]
        yt_ref[qi, hd] = (o[:D_V] * (1.0 / l)).astype(BF16)

    def scores(qi, hd):
        q = qt_ref[qi, hd]
        m = None
        for c in range(n_chunks):
            m = score_chunk(q, hd, c, m)
        return m

    def overlapped(qi_scores, hd_scores, qi_values, hd_values, m_values):
        q = qt_ref[qi_scores, hd_scores]
        m, o = None, None
        for c in range(n_chunks):
            m = score_chunk(q, hd_scores, c, m)
            o = value_chunk(hd_values, c, m_values, o)
        emit(qi_values, hd_values, o)
        return m

    def step(qi, m):
        for hd in range(heads - 1):
            m = overlapped(qi, hd + 1, qi, hd, m)
        return overlapped(qi + 1, 0, qi, heads - 1, m)

    m = jax.lax.fori_loop(0, nq - 1, step, scores(0, 0))
    for hd in range(heads - 1):
        m = overlapped(nq - 1, hd + 1, nq - 1, hd, m)
    o = None
    for c in range(n_chunks):
        o = value_chunk(heads - 1, c, m, o)
    emit(nq - 1, heads - 1, o)


def _attention(layer, qt, k, vt, cast_stacks, batch, seq):
    tq = Q_TILE
    nq = seq // tq
    n_groups = N_HEADS // HEADS_PER_STEP
    cast_in, cast_out, cast_shapes = _cast_plan(
        layer, cast_stacks, batch * n_groups, lambda b, g: b * n_groups + g)
    outs = pl.pallas_call(
        functools.partial(_attention_kernel, len(cast_stacks)),
        grid=(batch, n_groups),
        in_specs=[pl.BlockSpec((nq, HEADS_PER_STEP, HEAD_PAD, tq), lambda b, g: (b, g, 0, 0)),
                  pl.BlockSpec((HEADS_PER_STEP, seq, HEAD_PAD), lambda b, g: (g, b, 0)),
                  pl.BlockSpec((HEADS_PER_STEP, V_ROWS, seq), lambda b, g: (g, 0, b))] + cast_in,
        out_specs=[pl.BlockSpec((nq, HEADS_PER_STEP, D_V, tq), lambda b, g: (b, g, 0, 0))]
        + cast_out,
        out_shape=[jax.ShapeDtypeStruct((batch * nq, N_HEADS, D_V, tq), BF16)] + cast_shapes,
        scratch_shapes=[pltpu.VMEM((seq, tq), F32)] * 2,
        compiler_params=_params(2),
        name="attention",
    )(qt, k, vt, *cast_stacks)
    return outs[0], outs[1:]


def _conv3(buf_ref, w, n_rows):
    u = buf_ref[...]
    total = u.shape[0]
    full = (pltpu.roll(w[0:1] * u, 1, axis=0)
            + w[1:2] * u
            + pltpu.roll(w[2:3] * u, total - 1, axis=0))
    return full[HALO:HALO + n_rows]


def _edge_masks(tiles_per_seq):
    j = pl.program_id(0) % tiles_per_seq
    return (j != 0).astype(F32), (j != tiles_per_seq - 1).astype(F32)


def _merge_kernel(tiles_per_seq, x_ref, yt_ref, cb_ref, ccx_ref, ccx_prev_ref, ccx_next_ref,
                  gmix_ref, win_ref, bgate_ref, wconv_ref, wau_ref, wcu_ref, wo_ref,
                  out_ref, cs_ref):
    tm = x_ref.shape[0]
    keep_prev, keep_next = _edge_masks(tiles_per_seq)
    x = x_ref[...]
    h = _rms_rows(x, gmix_ref[...]).astype(BF16)

    two = 2 * HALO
    cs_ref[0:HALO, :] = ccx_prev_ref[...].astype(F32)[two - HALO:] * keep_prev
    cs_ref[HALO:HALO + tm, :] = ccx_ref[...].astype(F32)
    cs_ref[HALO + tm:, :] = ccx_next_ref[...].astype(F32)[:HALO] * keep_next
    y_conv = (cb_ref[...].astype(F32) * _conv3(cs_ref, wconv_ref[...], tm)).astype(BF16)

    y_attn = jnp.concatenate(
        [yt_ref[j].reshape(N_HEADS * D_V, Q_TILE).astype(F32).T for j in range(Q_PER_ROW_TILE)],
        axis=0).astype(BF16)
    gates = jax.nn.sigmoid(_dot_nt(h, win_ref[N_LATENT + N_CONV_IN:]) + bgate_ref[...])
    merged = (gates[:, :D_MODEL] * _dot(y_attn, wau_ref[...])
              + gates[:, D_MODEL:] * _dot(y_conv, wcu_ref[...]))
    out_ref[...] = x + _dot(merged.astype(BF16), wo_ref[...])


def _merge(layer, x2d, yt, cb, ccx, w, branch_weights, seq):
    n_tokens = x2d.shape[0]
    tm = ROW_TILE
    two = 2 * HALO
    per_tile = tm // two
    last_blk = n_tokens // two - 1
    row = lambda width: pl.BlockSpec((tm, width), lambda i: (i, 0))
    return pl.pallas_call(
        functools.partial(_merge_kernel, seq // tm),
        grid=(n_tokens // tm,),
        in_specs=[row(D_MODEL),
                  pl.BlockSpec((Q_PER_ROW_TILE, N_HEADS, D_V, Q_TILE), lambda i: (i, 0, 0, 0)),
                  row(CONV_WIDTH), row(CONV_WIDTH),
                  pl.BlockSpec((two, CONV_WIDTH), lambda i: (jnp.maximum(i * per_tile - 1, 0), 0)),
                  pl.BlockSpec((two, CONV_WIDTH),
                               lambda i: (jnp.minimum((i + 1) * per_tile, last_blk), 0)),
                  _layer_spec(layer, 1, D_MODEL),
                  _layer_spec(layer, N_LATENT + N_CONV_IN + N_GATES, D_MODEL),
                  _layer_spec(layer, 1, N_GATES),
                  _layer_spec(layer, 3, CONV_WIDTH),
                  _whole_spec(N_HEADS * D_V, D_MODEL),
                  _whole_spec(CONV_WIDTH, D_MODEL),
                  _whole_spec(D_MODEL, D_MODEL)],
        out_specs=row(D_MODEL),
        out_shape=jax.ShapeDtypeStruct((n_tokens, D_MODEL), F32),
        scratch_shapes=[pltpu.VMEM((tm + 2 * HALO, CONV_WIDTH), F32)],
        compiler_params=_params(1),
        name="merge",
    )(x2d, yt, cb, ccx, ccx, ccx, w["g_mix"], w["w_in"], w["b_gate"], w["w_conv"],
      *branch_weights)


def _ffn_kernel(tiles_per_seq, x_ref, xprev_ref, xnext_ref, p_ref, gffn_ref, wup_ref, wfc_ref,
                wdn_ref, gple_ref, wpg_ref, wple_ref, out_ref, lhs_ref, ua0_ref, uv0_ref,
                ua1_ref, uv1_ref, act_ref):
    tm = x_ref.shape[0]
    out_ref[...] = _dot(p_ref[...].astype(BF16), wple_ref[...])
    keep_prev, keep_next = _edge_masks(tiles_per_seq)
    x = x_ref[...]
    gffn = gffn_ref[...]
    lhs_ref[...] = jnp.concatenate(
        [_rms_rows(xprev_ref[...], gffn) * keep_prev,
         _rms_rows(x, gffn),
         _rms_rows(xnext_ref[...], gffn) * keep_next], axis=0).astype(BF16)
    u_bufs = ((ua0_ref, uv0_ref), (ua1_ref, uv1_ref))

    def cols(j, half):
        return pl.ds(half * D_FF + j * FF_CHUNK, FF_CHUNK)

    def up(j):
        lhs = lhs_ref[...]
        ua_ref, uv_ref = u_bufs[j % 2]
        ua_ref[...] = _dot(lhs, wup_ref[:, cols(j, 0)])
        uv_ref[...] = _dot(lhs, wup_ref[:, cols(j, 1)])

    def gate(j):
        ua_ref, uv_ref = u_bufs[j % 2]
        a = _conv3(ua_ref, wfc_ref[:, cols(j, 0)], tm)
        v = _conv3(uv_ref, wfc_ref[:, cols(j, 1)], tm)
        act_ref[:, cols(j, 0)] = (a * jax.nn.sigmoid(a) * v).astype(BF16)

    def down(first, last):
        span = pl.ds(first * FF_CHUNK, (last - first) * FF_CHUNK)
        return _dot(act_ref[:, span], wdn_ref[span, :])

    y = None
    up(0)
    for j in range(N_FF_CHUNKS):
        if j + 1 < N_FF_CHUNKS:
            up(j + 1)
        gate(j)
        if j + 1 in DOWN_SPLITS:
            first = ([0] + list(DOWN_SPLITS))[DOWN_SPLITS.index(j + 1)]
            part = down(first, j + 1)
            y = part if y is None else y + part

    x2 = x + y
    h3 = _rms_rows(x2, gple_ref[...]).astype(BF16)
    pg = jax.nn.sigmoid(_dot(h3, wpg_ref[...]))
    out_ref[...] = x2 + pg * out_ref[...]


def _ffn(layer, x2d, p3d, w, mixer_weights, seq):
    w_up_b, w_down_b, w_ple_gate_b, w_ple_b = mixer_weights
    n_tokens = x2d.shape[0]
    tm = FFN_TILE
    per_tile = tm // HALO
    last_blk = n_tokens // HALO - 1
    row = lambda width: pl.BlockSpec((tm, width), lambda i: (i, 0))
    return pl.pallas_call(
        functools.partial(_ffn_kernel, seq // tm),
        grid=(n_tokens // tm,),
        in_specs=[row(D_MODEL),
                  pl.BlockSpec((HALO, D_MODEL), lambda i: (jnp.maximum(i * per_tile - 1, 0), 0)),
                  pl.BlockSpec((HALO, D_MODEL),
                               lambda i: (jnp.minimum((i + 1) * per_tile, last_blk), 0)),
                  pl.BlockSpec((None, tm, PLE_DIM), lambda i: (layer, i, 0)),
                  _layer_spec(layer, 1, D_MODEL),
                  _whole_spec(D_MODEL, 2 * D_FF),
                  _layer_spec(layer, 3, 2 * D_FF),
                  _whole_spec(D_FF, D_MODEL),
                  _layer_spec(layer, 1, D_MODEL),
                  _whole_spec(D_MODEL, D_MODEL),
                  _whole_spec(PLE_DIM, D_MODEL)],
        out_specs=row(D_MODEL),
        out_shape=jax.ShapeDtypeStruct((n_tokens, D_MODEL), F32),
        scratch_shapes=[pltpu.VMEM((tm + 2 * HALO, D_MODEL), BF16)]
        + [pltpu.VMEM((tm + 2 * HALO, FF_CHUNK), F32)] * 4
        + [pltpu.VMEM((tm, D_FF), BF16)],
        compiler_params=_params(1),
        name="ffn",
    )(x2d, x2d, x2d, p3d, w["g_ffn"], w_up_b, w["w_ffn_conv"], w_down_b, w["g_ple"],
      w_ple_gate_b, w_ple_b)


def _prep_weights(w_in, b_gate, g_mix, g_q_lat, w_uq, g_kv_lat, w_ukv, g_q_head, g_k_head,
                  w_conv, g_ffn, w_ffn_conv, g_ple):
    depth = w_in.shape[0]
    as_row = lambda g: g.reshape(depth, 1, g.shape[-1])
    as_col = lambda g: g.reshape(depth, g.shape[-1], 1)

    uq = w_uq.reshape(depth, Q_LORA, N_HEADS, D_QK)
    uq = jnp.pad(uq, ((0, 0), (0, 0), (0, 0), (0, HEAD_PAD - D_QK)))
    w_uq_t = uq.reshape(depth, Q_LORA, N_HEADS * HEAD_PAD).transpose(0, 2, 1).astype(BF16)

    ukv = w_ukv.reshape(depth, KV_LORA, N_HEADS, D_NOPE + D_V)
    uk = ukv[..., :D_NOPE].reshape(depth, KV_LORA, N_HEADS * D_NOPE)
    uv = ukv[..., D_NOPE:].reshape(depth, KV_LORA, N_HEADS * D_V)
    w_ukv_t = jnp.concatenate([uk, uv], axis=2).transpose(0, 2, 1).astype(BF16)

    return {
        "g_mix": as_row(g_mix),
        "w_in": w_in.transpose(0, 2, 1).astype(BF16),
        "b_gate": as_row(b_gate),
        "g_q_lat": as_col(g_q_lat),
        "w_uq_t": w_uq_t,
        "g_kv_lat": as_col(g_kv_lat),
        "w_ukv_t": w_ukv_t,
        "g_q_head": as_col(g_q_head * (D_QK ** -0.5 * LOG2_E)),
        "g_k_head": as_col(g_k_head),
        "w_conv": w_conv,
        "g_ffn": as_row(g_ffn),
        "w_ffn_conv": w_ffn_conv,
        "g_ple": as_row(g_ple),
    }


def kernel(x, p, positions, w_in, b_gate, g_mix, g_q_lat, w_uq, g_kv_lat, w_ukv, g_q_head,
           g_k_head, w_attn_up, w_conv, w_conv_up, w_o, g_ffn, w_up, w_ffn_conv, w_down,
           g_ple, w_ple_gate, w_ple):
    batch, seq, d_model = x.shape
    depth = w_in.shape[0]
    assert d_model == D_MODEL and seq % ROW_TILE == 0 and seq % FFN_TILE == 0
    n_tokens = batch * seq
    tabs = _rope_tables(positions, n_tokens, seq)
    w = _prep_weights(w_in, b_gate, g_mix, g_q_lat, w_uq, g_kv_lat, w_ukv, g_q_head, g_k_head,
                      w_conv, g_ffn, w_ffn_conv, g_ple)
    x2d = x.reshape(n_tokens, D_MODEL)
    p3d = p.reshape(depth, n_tokens, PLE_DIM)
    for layer in range(depth):
        (qt, k, vt, cb, ccx), branch_weights = _inproj(
            layer, x2d, tabs, w, (w_attn_up, w_conv_up, w_o))
        yt, mixer_weights = _attention(
            layer, qt, k, vt, (w_up, w_down, w_ple_gate, w_ple), batch, seq)
        x2d = _merge(layer, x2d, yt, cb, ccx, w, branch_weights, seq)
        x2d = _ffn(layer, x2d, p3d, w, mixer_weights, seq)
    return x2d.reshape(batch, seq, D_MODEL)
```

```python
import functools

import numpy as np
import jax
import jax.numpy as jnp
from jax.experimental import pallas as pl
from jax.experimental.pallas import tpu as pltpu

D_MODEL = 1024
N_HEADS = 8
D_NOPE = 64
D_ROPE = 32
D_QK = D_NOPE + D_ROPE
D_V = 64
Q_LORA = 256
KV_LORA = 128
CONV_WIDTH = 512
D_FF = 2816
PLE_DIM = 256
EPS = 1e-6
ROPE_THETA = 10000.0

HEAD_PAD = 128
HALF_ROPE = D_ROPE // 2
V_ROWS = D_V + 16
N_LATENT = Q_LORA + KV_LORA + D_ROPE
N_CONV_IN = 3 * CONV_WIDTH
N_GATES = 2 * D_MODEL
HALO = 8
BF16_SUBLANES = 16
FF_CHUNK = 256
N_FF_CHUNKS = D_FF // FF_CHUNK
DOWN_SPLITS = (N_FF_CHUNKS,)
VMEM_LIMIT_BYTES = 56 * 1024 * 1024

ROW_TILE = 1024
Q_TILE = 512
Q_PER_ROW_TILE = ROW_TILE // Q_TILE
FFN_TILE = 1024
KEY_CHUNK = 256
HEADS_PER_STEP = 8
LOG2_E = 1.4426950408889634

BF16 = jnp.bfloat16
F32 = jnp.float32


def _dot(a, b):
    return jnp.dot(a, b, preferred_element_type=F32)


def _dot_nt(a, b):
    return jax.lax.dot_general(a, b, (((1,), (1,)), ((), ())), preferred_element_type=F32)


def _rms_rows(v, gain):
    return v * jax.lax.rsqrt(jnp.mean(v * v, axis=-1, keepdims=True) + EPS) * gain


def _rms_cols(v, gain):
    return v * jax.lax.rsqrt(jnp.mean(v * v, axis=0, keepdims=True) + EPS) * gain


def _rope_cols(t, cos, sin):
    t1, t2 = t[:HALF_ROPE], t[HALF_ROPE:]
    return t1 * cos - t2 * sin, t2 * cos + t1 * sin


def _cast_plan(layer, stacks, n_steps, step_of):
    in_specs, out_specs, out_shapes = [], [], []
    for stack in stacks:
        _, n_rows, n_cols = stack.shape
        slab = n_rows // n_steps
        assert slab * n_steps == n_rows and slab % BF16_SUBLANES == 0
        in_specs.append(pl.BlockSpec((None, slab, n_cols),
                                     lambda *idx: (layer, step_of(*idx), 0)))
        out_specs.append(pl.BlockSpec((slab, n_cols), lambda *idx: (step_of(*idx), 0)))
        out_shapes.append(jax.ShapeDtypeStruct((n_rows, n_cols), BF16))
    return in_specs, out_specs, out_shapes


def _cast_slabs(src_refs, dst_refs):
    for src_ref, dst_ref in zip(src_refs, dst_refs):
        dst_ref[...] = src_ref[...].astype(BF16)


def _whole_spec(rows, cols):
    return pl.BlockSpec((rows, cols), lambda *_: (0, 0), pipeline_mode=pl.Buffered(1))


def _layer_spec(layer, rows, cols):
    return pl.BlockSpec((None, rows, cols), lambda *_: (layer, 0, 0),
                        pipeline_mode=pl.Buffered(1))


def _params(n_axes):
    return pltpu.CompilerParams(
        dimension_semantics=("arbitrary",) * n_axes,
        vmem_limit_bytes=VMEM_LIMIT_BYTES)


def _rope_operands(positions, n_tokens):
    inv = ROPE_THETA ** (-np.arange(0, D_ROPE, 2, dtype=np.float32) / D_ROPE)
    inv = jnp.asarray(inv.astype(np.float32)).reshape(HALF_ROPE, 1)
    return positions.reshape(1, n_tokens), inv


def _inproj_kernel(n_cast, x_ref, gmix_ref, win_ref, gq_ref, wuq_ref, gkv_ref,
                   wukv_ref, gqh_ref, gkh_ref, pos_ref, inv_ref, *refs):
    cast_src, outs = refs[:n_cast], refs[n_cast:]
    qt_ref, k_ref, vt_ref, cb_ref, ccx_ref = outs[:5]
    cast_dst = outs[5:]
    _cast_slabs(cast_src, cast_dst)

    h = _rms_rows(x_ref[...], gmix_ref[...]).astype(BF16)

    conv_lo = N_LATENT
    zt = _dot_nt(win_ref[:N_LATENT], h)
    cb_ref[...] = _dot_nt(h, win_ref[conv_lo:conv_lo + CONV_WIDTH]).astype(BF16)

    cqn = _rms_cols(zt[:Q_LORA], gq_ref[...]).astype(BF16)
    qt = _dot(wuq_ref[...], cqn)
    ckvn = _rms_cols(zt[Q_LORA:Q_LORA + KV_LORA], gkv_ref[...]).astype(BF16)
    kvt = _dot(wukv_ref[...], ckvn)

    zcx = _dot_nt(h, win_ref[conv_lo + CONV_WIDTH:])
    ccx_ref[...] = (zcx[:, :CONV_WIDTH] * zcx[:, CONV_WIDTH:]).astype(BF16)

    ang = pos_ref[...].astype(F32) * inv_ref[...]
    cos = jnp.cos(ang)
    sin = jnp.sin(ang)
    gqh = gqh_ref[...]
    for hd in range(N_HEADS):
        blk = qt[hd * HEAD_PAD:(hd + 1) * HEAD_PAD]
        qn = _rms_cols(blk[:D_NOPE], gqh[:D_NOPE])
        qr = _rms_cols(blk[D_NOPE:D_QK], gqh[D_NOPE:])
        r1, r2 = _rope_cols(qr, cos, sin)
        full = jnp.concatenate([qn, r1, r2, blk[D_QK:]], axis=0)
        full = full.astype(BF16)
        for j in range(Q_PER_ROW_TILE):
            qt_ref[j, hd] = full[:, j * Q_TILE:(j + 1) * Q_TILE]

    for hd in range(N_HEADS):
        lo = N_HEADS * D_NOPE + hd * D_V
        vt_ref[hd, 0:D_V, :] = kvt[lo:lo + D_V].astype(BF16)
        vt_ref[hd, D_V:, :] = jnp.ones((V_ROWS - D_V, kvt.shape[1]), BF16)

    gkh = gkh_ref[...]
    kr = _rms_cols(zt[Q_LORA + KV_LORA:N_LATENT], gkh[D_NOPE:])
    kr1, kr2 = _rope_cols(kr, cos, sin)
    pad = jnp.zeros((HEAD_PAD - D_QK, kr.shape[1]), F32)
    for hd in range(N_HEADS):
        kn = _rms_cols(kvt[hd * D_NOPE:(hd + 1) * D_NOPE], gkh[:D_NOPE])
        full = jnp.concatenate([kn, kr1, kr2, pad], axis=0)
        k_ref[hd] = full.T.astype(BF16)


def _inproj(layer, x2d, tabs, w, cast_stacks):
    n_tokens = x2d.shape[0]
    tm = ROW_TILE
    pos, inv = tabs
    cast_in, cast_out, cast_shapes = _cast_plan(layer, cast_stacks, n_tokens // tm, lambda i: i)
    row = lambda width: pl.BlockSpec((tm, width), lambda i: (i, 0))
    outs = [
        jax.ShapeDtypeStruct((n_tokens // Q_TILE, N_HEADS, HEAD_PAD, Q_TILE), BF16),
        jax.ShapeDtypeStruct((N_HEADS, n_tokens, HEAD_PAD), BF16),
        jax.ShapeDtypeStruct((N_HEADS, V_ROWS, n_tokens), BF16),
        jax.ShapeDtypeStruct((n_tokens, CONV_WIDTH), BF16),
        jax.ShapeDtypeStruct((n_tokens, CONV_WIDTH), BF16),
    ]
    results = pl.pallas_call(
        functools.partial(_inproj_kernel, len(cast_stacks)),
        grid=(n_tokens // tm,),
        in_specs=[row(D_MODEL),
                  _layer_spec(layer, 1, D_MODEL),
                  _layer_spec(layer, N_LATENT + N_CONV_IN, D_MODEL),
                  _layer_spec(layer, Q_LORA, 1),
                  _layer_spec(layer, N_HEADS * HEAD_PAD, Q_LORA),
                  _layer_spec(layer, KV_LORA, 1),
                  _layer_spec(layer, N_HEADS * (D_NOPE + D_V), KV_LORA),
                  _layer_spec(layer, D_QK, 1),
                  _layer_spec(layer, D_QK, 1),
                  pl.BlockSpec((1, tm), lambda i: (0, i)),
                  pl.BlockSpec((HALF_ROPE, 1), lambda i: (0, 0))] + cast_in,
        out_specs=[pl.BlockSpec((Q_PER_ROW_TILE, N_HEADS, HEAD_PAD, Q_TILE), lambda i: (i, 0, 0, 0)),
                   pl.BlockSpec((N_HEADS, tm, HEAD_PAD), lambda i: (0, i, 0)),
                   pl.BlockSpec((N_HEADS, V_ROWS, tm), lambda i: (0, 0, i)),
                   row(CONV_WIDTH), row(CONV_WIDTH)] + cast_out,
        out_shape=outs + cast_shapes,
        compiler_params=_params(1),
        name="inproj",
    )(x2d, w["g_mix"], w["w_in"], w["g_q_lat"], w["w_uq_t"], w["g_kv_lat"],
      w["w_ukv_t"], w["g_q_head"], w["g_k_head"], pos, inv, *cast_stacks)
    return results[:5], results[5:]


def _attention_kernel(n_cast, qt_ref, k_ref, vt_ref, *refs):
    cast_src, (yt_ref, *cast_dst) = refs[:n_cast], refs[n_cast:2 * n_cast + 1]
    s0_ref, s1_ref = refs[2 * n_cast + 1:]
    _cast_slabs(cast_src, cast_dst)
    nq, heads = qt_ref.shape[:2]
    s_refs = (s0_ref, s1_ref)

    seq = k_ref.shape[1]
    n_chunks = seq // KEY_CHUNK
    rows = lambda c: pl.ds(c * KEY_CHUNK, KEY_CHUNK)

    def score_chunk(q, hd, c, m):
        s = _dot(k_ref[hd, rows(c), :], q)
        s_refs[hd % 2][rows(c), :] = s
        part = jnp.max(s, axis=0, keepdims=True)
        return part if m is None else jnp.maximum(m, part)

    def value_chunk(hd, c, m, o):
        p = jnp.exp2(s_refs[hd % 2][rows(c), :] - m).astype(BF16)
        part = _dot(vt_ref[hd, :, rows(c)], p)
        return part if o is None else o + part

    def emit(qi, hd, o):
        l = o[D_V:D_V + 1]
        yt_ref[qi, hd] = (o[:D_V] * (1.0 / l)).astype(BF16)

    def scores(qi, hd):
        q = qt_ref[qi, hd]
        m = None
        for c in range(n_chunks):
            m = score_chunk(q, hd, c, m)
        return m

    def overlapped(qi_scores, hd_scores, qi_values, hd_values, m_values):
        q = qt_ref[qi_scores, hd_scores]
        m, o = None, None
        for c in range(n_chunks):
            m = score_chunk(q, hd_scores, c, m)
            o = value_chunk(hd_values, c, m_values, o)
        emit(qi_values, hd_values, o)
        return m

    def step(qi, m):
        for hd in range(heads - 1):
            m = overlapped(qi, hd + 1, qi, hd, m)
        return overlapped(qi + 1, 0, qi, heads - 1, m)

    m = jax.lax.fori_loop(0, nq - 1, step, scores(0, 0))
    for hd in range(heads - 1):
        m = overlapped(nq - 1, hd + 1, nq - 1, hd, m)
    o = None
    for c in range(n_chunks):
        o = value_chunk(heads - 1, c, m, o)
    emit(nq - 1, heads - 1, o)


def _attention(layer, qt, k, vt, cast_stacks, batch, seq):
    tq = Q_TILE
    nq = seq // tq
    n_groups = N_HEADS // HEADS_PER_STEP
    cast_in, cast_out, cast_shapes = _cast_plan(
        layer, cast_stacks, batch * n_groups, lambda b, g: b * n_groups + g)
    outs = pl.pallas_call(
        functools.partial(_attention_kernel, len(cast_stacks)),
        grid=(batch, n_groups),
        in_specs=[pl.BlockSpec((nq, HEADS_PER_STEP, HEAD_PAD, tq), lambda b, g: (b, g, 0, 0)),
                  pl.BlockSpec((HEADS_PER_STEP, seq, HEAD_PAD), lambda b, g: (g, b, 0)),
                  pl.BlockSpec((HEADS_PER_STEP, V_ROWS, seq), lambda b, g: (g, 0, b))] + cast_in,
        out_specs=[pl.BlockSpec((nq, HEADS_PER_STEP, D_V, tq), lambda b, g: (b, g, 0, 0))]
        + cast_out,
        out_shape=[jax.ShapeDtypeStruct((batch * nq, N_HEADS, D_V, tq), BF16)] + cast_shapes,
        scratch_shapes=[pltpu.VMEM((seq, tq), F32)] * 2,
        compiler_params=_params(2),
        name="attention",
    )(qt, k, vt, *cast_stacks)
    return outs[0], outs[1:]


def _conv3(buf_ref, w, n_rows):
    u = buf_ref[...]
    total = u.shape[0]
    full = (pltpu.roll(w[0:1] * u, 1, axis=0)
            + w[1:2] * u
            + pltpu.roll(w[2:3] * u, total - 1, axis=0))
    return full[HALO:HALO + n_rows]


def _edge_masks(tiles_per_seq):
    j = pl.program_id(0) % tiles_per_seq
    return (j != 0).astype(F32), (j != tiles_per_seq - 1).astype(F32)


def _merge_kernel(tiles_per_seq, x_ref, yt_ref, cb_ref, ccx_ref, ccx_prev_ref, ccx_next_ref,
                  gmix_ref, win_ref, bgate_ref, wconv_ref, wau_ref, wcu_ref, wo_ref,
                  out_ref, cs_ref):
    tm = x_ref.shape[0]
    keep_prev, keep_next = _edge_masks(tiles_per_seq)
    x = x_ref[...]
    h = _rms_rows(x, gmix_ref[...]).astype(BF16)

    two = 2 * HALO
    cs_ref[0:HALO, :] = ccx_prev_ref[...].astype(F32)[two - HALO:] * keep_prev
    cs_ref[HALO:HALO + tm, :] = ccx_ref[...].astype(F32)
    cs_ref[HALO + tm:, :] = ccx_next_ref[...].astype(F32)[:HALO] * keep_next
    y_conv = (cb_ref[...].astype(F32) * _conv3(cs_ref, wconv_ref[...], tm)).astype(BF16)

    y_attn = jnp.concatenate(
        [yt_ref[j].reshape(N_HEADS * D_V, Q_TILE).astype(F32).T for j in range(Q_PER_ROW_TILE)],
        axis=0).astype(BF16)
    gates = jax.nn.sigmoid(_dot_nt(h, win_ref[N_LATENT + N_CONV_IN:]) + bgate_ref[...])
    merged = (gates[:, :D_MODEL] * _dot(y_attn, wau_ref[...])
              + gates[:, D_MODEL:] * _dot(y_conv, wcu_ref[...]))
    out_ref[...] = x + _dot(merged.astype(BF16), wo_ref[...])


def _merge(layer, x2d, yt, cb, ccx, w, branch_weights, seq):
    n_tokens = x2d.shape[0]
    tm = ROW_TILE
    two = 2 * HALO
    per_tile = tm // two
    last_blk = n_tokens // two - 1
    row = lambda width: pl.BlockSpec((tm, width), lambda i: (i, 0))
    return pl.pallas_call(
        functools.partial(_merge_kernel, seq // tm),
        grid=(n_tokens // tm,),
        in_specs=[row(D_MODEL),
                  pl.BlockSpec((Q_PER_ROW_TILE, N_HEADS, D_V, Q_TILE), lambda i: (i, 0, 0, 0)),
                  row(CONV_WIDTH), row(CONV_WIDTH),
                  pl.BlockSpec((two, CONV_WIDTH), lambda i: (jnp.maximum(i * per_tile - 1, 0), 0)),
                  pl.BlockSpec((two, CONV_WIDTH),
                               lambda i: (jnp.minimum((i + 1) * per_tile, last_blk), 0)),
                  _layer_spec(layer, 1, D_MODEL),
                  _layer_spec(layer, N_LATENT + N_CONV_IN + N_GATES, D_MODEL),
                  _layer_spec(layer, 1, N_GATES),
                  _layer_spec(layer, 3, CONV_WIDTH),
                  _whole_spec(N_HEADS * D_V, D_MODEL),
                  _whole_spec(CONV_WIDTH, D_MODEL),
                  _whole_spec(D_MODEL, D_MODEL)],
        out_specs=row(D_MODEL),
        out_shape=jax.ShapeDtypeStruct((n_tokens, D_MODEL), F32),
        scratch_shapes=[pltpu.VMEM((tm + 2 * HALO, CONV_WIDTH), F32)],
        compiler_params=_params(1),
        name="merge",
    )(x2d, yt, cb, ccx, ccx, ccx, w["g_mix"], w["w_in"], w["b_gate"], w["w_conv"],
      *branch_weights)


def _ffn_kernel(tiles_per_seq, x_ref, xprev_ref, xnext_ref, p_ref, gffn_ref, wup_ref, wfc_ref,
                wdn_ref, gple_ref, wpg_ref, wple_ref, out_ref, lhs_ref, ua0_ref, uv0_ref,
                ua1_ref, uv1_ref, act_ref):
    tm = x_ref.shape[0]
    out_ref[...] = _dot(p_ref[...].astype(BF16), wple_ref[...])
    keep_prev, keep_next = _edge_masks(tiles_per_seq)
    x = x_ref[...]
    gffn = gffn_ref[...]
    lhs_ref[...] = jnp.concatenate(
        [_rms_rows(xprev_ref[...], gffn) * keep_prev,
         _rms_rows(x, gffn),
         _rms_rows(xnext_ref[...], gffn) * keep_next], axis=0).astype(BF16)
    u_bufs = ((ua0_ref, uv0_ref), (ua1_ref, uv1_ref))

    def cols(j, half):
        return pl.ds(half * D_FF + j * FF_CHUNK, FF_CHUNK)

    def up(j):
        lhs = lhs_ref[...]
        ua_ref, uv_ref = u_bufs[j % 2]
        ua_ref[...] = _dot(lhs, wup_ref[:, cols(j, 0)])
        uv_ref[...] = _dot(lhs, wup_ref[:, cols(j, 1)])

    def gate(j):
        ua_ref, uv_ref = u_bufs[j % 2]
        a = _conv3(ua_ref, wfc_ref[:, cols(j, 0)], tm)
        v = _conv3(uv_ref, wfc_ref[:, cols(j, 1)], tm)
        act_ref[:, cols(j, 0)] = (a * jax.nn.sigmoid(a) * v).astype(BF16)

    def down(first, last):
        span = pl.ds(first * FF_CHUNK, (last - first) * FF_CHUNK)
        return _dot(act_ref[:, span], wdn_ref[span, :])

    y = None
    up(0)
    for j in range(N_FF_CHUNKS):
        if j + 1 < N_FF_CHUNKS:
            up(j + 1)
        gate(j)
        if j + 1 in DOWN_SPLITS:
            first = ([0] + list(DOWN_SPLITS))[DOWN_SPLITS.index(j + 1)]
            part = down(first, j + 1)
            y = part if y is None else y + part

    x2 = x + y
    h3 = _rms_rows(x2, gple_ref[...]).astype(BF16)
    pg = jax.nn.sigmoid(_dot(h3, wpg_ref[...]))
    out_ref[...] = x2 + pg * out_ref[...]


def _ffn(layer, x2d, p3d, w, mixer_weights, seq):
    w_up_b, w_down_b, w_ple_gate_b, w_ple_b = mixer_weights
    n_tokens = x2d.shape[0]
    tm = FFN_TILE
    per_tile = tm // HALO
    last_blk = n_tokens // HALO - 1
    row = lambda width: pl.BlockSpec((tm, width), lambda i: (i, 0))
    return pl.pallas_call(
        functools.partial(_ffn_kernel, seq // tm),
        grid=(n_tokens // tm,),
        in_specs=[row(D_MODEL),
                  pl.BlockSpec((HALO, D_MODEL), lambda i: (jnp.maximum(i * per_tile - 1, 0), 0)),
                  pl.BlockSpec((HALO, D_MODEL),
                               lambda i: (jnp.minimum((i + 1) * per_tile, last_blk), 0)),
                  pl.BlockSpec((None, tm, PLE_DIM), lambda i: (layer, i, 0)),
                  _layer_spec(layer, 1, D_MODEL),
                  _whole_spec(D_MODEL, 2 * D_FF),
                  _layer_spec(layer, 3, 2 * D_FF),
                  _whole_spec(D_FF, D_MODEL),
                  _layer_spec(layer, 1, D_MODEL),
                  _whole_spec(D_MODEL, D_MODEL),
                  _whole_spec(PLE_DIM, D_MODEL)],
        out_specs=row(D_MODEL),
        out_shape=jax.ShapeDtypeStruct((n_tokens, D_MODEL), F32),
        scratch_shapes=[pltpu.VMEM((tm + 2 * HALO, D_MODEL), BF16)]
        + [pltpu.VMEM((tm + 2 * HALO, FF_CHUNK), F32)] * 4
        + [pltpu.VMEM((tm, D_FF), BF16)],
        compiler_params=_params(1),
        name="ffn",
    )(x2d, x2d, x2d, p3d, w["g_ffn"], w_up_b, w["w_ffn_conv"], w_down_b, w["g_ple"],
      w_ple_gate_b, w_ple_b)


def _prep_weights(w_in, b_gate, g_mix, g_q_lat, w_uq, g_kv_lat, w_ukv, g_q_head, g_k_head,
                  w_conv, g_ffn, w_ffn_conv, g_ple):
    depth = w_in.shape[0]
    as_row = lambda g: g.reshape(depth, 1, g.shape[-1])
    as_col = lambda g: g.reshape(depth, g.shape[-1], 1)

    uq = w_uq.reshape(depth, Q_LORA, N_HEADS, D_QK)
    uq = jnp.pad(uq, ((0, 0), (0, 0), (0, 0), (0, HEAD_PAD - D_QK)))
    w_uq_t = uq.reshape(depth, Q_LORA, N_HEADS * HEAD_PAD).transpose(0, 2, 1).astype(BF16)

    ukv = w_ukv.reshape(depth, KV_LORA, N_HEADS, D_NOPE + D_V)
    uk = ukv[..., :D_NOPE].reshape(depth, KV_LORA, N_HEADS * D_NOPE)
    uv = ukv[..., D_NOPE:].reshape(depth, KV_LORA, N_HEADS * D_V)
    w_ukv_t = jnp.concatenate([uk, uv], axis=2).transpose(0, 2, 1).astype(BF16)

    return {
        "g_mix": as_row(g_mix),
        "w_in": w_in.transpose(0, 2, 1).astype(BF16),
        "b_gate": as_row(b_gate),
        "g_q_lat": as_col(g_q_lat),
        "w_uq_t": w_uq_t,
        "g_kv_lat": as_col(g_kv_lat),
        "w_ukv_t": w_ukv_t,
        "g_q_head": as_col(g_q_head * (D_QK ** -0.5 * LOG2_E)),
        "g_k_head": as_col(g_k_head),
        "w_conv": w_conv,
        "g_ffn": as_row(g_ffn),
        "w_ffn_conv": w_ffn_conv,
        "g_ple": as_row(g_ple),
    }


def kernel(x, p, positions, w_in, b_gate, g_mix, g_q_lat, w_uq, g_kv_lat, w_ukv, g_q_head,
           g_k_head, w_attn_up, w_conv, w_conv_up, w_o, g_ffn, w_up, w_ffn_conv, w_down,
           g_ple, w_ple_gate, w_ple):
    batch, seq, d_model = x.shape
    depth = w_in.shape[0]
    assert d_model == D_MODEL and seq % ROW_TILE == 0 and seq % FFN_TILE == 0
    n_tokens = batch * seq
    tabs = _rope_operands(positions, n_tokens)
    w = _prep_weights(w_in, b_gate, g_mix, g_q_lat, w_uq, g_kv_lat, w_ukv, g_q_head, g_k_head,
                      w_conv, g_ffn, w_ffn_conv, g_ple)
    x2d = x.reshape(n_tokens, D_MODEL)
    p3d = p.reshape(depth, n_tokens, PLE_DIM)
    for layer in range(depth):
        (qt, k, vt, cb, ccx), branch_weights = _inproj(
            layer, x2d, tabs, w, (w_attn_up, w_conv_up, w_o))
        yt, mixer_weights = _attention(
            layer, qt, k, vt, (w_up, w_down, w_ple_gate, w_ple), batch, seq)
        x2d = _merge(layer, x2d, yt, cb, ccx, w, branch_weights, seq)
        x2d = _ffn(layer, x2d, p3d, w, mixer_weights, seq)
    return x2d.reshape(batch, seq, D_MODEL)
```

```python
import functools

import numpy as np
import jax
import jax.numpy as jnp
from jax.experimental import pallas as pl
from jax.experimental.pallas import tpu as pltpu

D_MODEL = 1024
N_HEADS = 8
D_NOPE = 64
D_ROPE = 32
D_QK = D_NOPE + D_ROPE
D_V = 64
Q_LORA = 256
KV_LORA = 128
CONV_WIDTH = 512
D_FF = 2816
PLE_DIM = 256
EPS = 1e-6
ROPE_THETA = 10000.0

HEAD_PAD = 128
HALF_ROPE = D_ROPE // 2
V_ROWS = D_V + 16
N_LATENT = Q_LORA + KV_LORA + D_ROPE
N_CONV_IN = 3 * CONV_WIDTH
N_GATES = 2 * D_MODEL
HALO = 8
BF16_SUBLANES = 16
FF_CHUNK = 256
N_FF_CHUNKS = D_FF // FF_CHUNK
DOWN_SPLITS = (N_FF_CHUNKS,)
VMEM_LIMIT_BYTES = 56 * 1024 * 1024

ROW_TILE = 1024
Q_TILE = 512
Q_PER_ROW_TILE = ROW_TILE // Q_TILE
FFN_TILE = 1024
KEY_CHUNK = 256
HEADS_PER_STEP = 8
LOG2_E = 1.4426950408889634

BF16 = jnp.bfloat16
F32 = jnp.float32


def _dot(a, b):
    return jnp.dot(a, b, preferred_element_type=F32)


def _dot_nt(a, b):
    return jax.lax.dot_general(a, b, (((1,), (1,)), ((), ())), preferred_element_type=F32)


def _rms_rows(v, gain):
    return v * jax.lax.rsqrt(jnp.mean(v * v, axis=-1, keepdims=True) + EPS) * gain


def _rms_cols(v, gain):
    return v * jax.lax.rsqrt(jnp.mean(v * v, axis=0, keepdims=True) + EPS) * gain


def _rope_cols(t, cos, sin):
    t1, t2 = t[:HALF_ROPE], t[HALF_ROPE:]
    return t1 * cos - t2 * sin, t2 * cos + t1 * sin


def _cast_plan(layer, stacks, n_steps, step_of):
    in_specs, out_specs, out_shapes = [], [], []
    for stack in stacks:
        _, n_rows, n_cols = stack.shape
        slab = n_rows // n_steps
        assert slab * n_steps == n_rows and slab % BF16_SUBLANES == 0
        in_specs.append(pl.BlockSpec((None, slab, n_cols),
                                     lambda *idx: (layer, step_of(*idx), 0)))
        out_specs.append(pl.BlockSpec((slab, n_cols), lambda *idx: (step_of(*idx), 0)))
        out_shapes.append(jax.ShapeDtypeStruct((n_rows, n_cols), BF16))
    return in_specs, out_specs, out_shapes


def _cast_slabs(src_refs, dst_refs):
    for src_ref, dst_ref in zip(src_refs, dst_refs):
        dst_ref[...] = src_ref[...].astype(BF16)


def _whole_spec(rows, cols):
    return pl.BlockSpec((rows, cols), lambda *_: (0, 0), pipeline_mode=pl.Buffered(1))


def _layer_spec(layer, rows, cols):
    return pl.BlockSpec((None, rows, cols), lambda *_: (layer, 0, 0),
                        pipeline_mode=pl.Buffered(1))


def _params(n_axes):
    return pltpu.CompilerParams(
        dimension_semantics=("arbitrary",) * n_axes,
        vmem_limit_bytes=VMEM_LIMIT_BYTES)


def _rope_table_kernel(pos_ref, inv_ref, cos_ref, sin_ref):
    ang = pos_ref[...].astype(F32) * inv_ref[...]
    cos_ref[...] = jnp.cos(ang)
    sin_ref[...] = jnp.sin(ang)


def _rope_tables(positions, n_tokens, seq):
    inv = ROPE_THETA ** (-np.arange(0, D_ROPE, 2, dtype=np.float32) / D_ROPE)
    inv = jnp.asarray(inv.astype(np.float32)).reshape(HALF_ROPE, 1)
    pos = positions.reshape(1, n_tokens)
    out = jax.ShapeDtypeStruct((HALF_ROPE, n_tokens), F32)
    return pl.pallas_call(
        _rope_table_kernel,
        grid=(n_tokens // seq,),
        in_specs=[pl.BlockSpec((1, seq), lambda i: (0, i)),
                  pl.BlockSpec((HALF_ROPE, 1), lambda i: (0, 0))],
        out_specs=[pl.BlockSpec((HALF_ROPE, seq), lambda i: (0, i))] * 2,
        out_shape=[out, out],
        compiler_params=_params(1),
        name="rope_tables",
    )(pos, inv)


def _inproj_kernel(n_cast, x_ref, gmix_ref, win_ref, gq_ref, wuq_ref, gkv_ref,
                   wukv_ref, gqh_ref, gkh_ref, cos_ref, sin_ref, *refs):
    cast_src, outs = refs[:n_cast], refs[n_cast:]
    qt_ref, k_ref, vt_ref, cb_ref, ccx_ref = outs[:5]
    cast_dst = outs[5:]
    _cast_slabs(cast_src, cast_dst)

    h = _rms_rows(x_ref[...], gmix_ref[...]).astype(BF16)

    conv_lo = N_LATENT
    zt = _dot_nt(win_ref[:N_LATENT], h)
    cb_ref[...] = _dot_nt(h, win_ref[conv_lo:conv_lo + CONV_WIDTH]).astype(BF16)

    cqn = _rms_cols(zt[:Q_LORA], gq_ref[...]).astype(BF16)
    qt = _dot(wuq_ref[...], cqn)
    ckvn = _rms_cols(zt[Q_LORA:Q_LORA + KV_LORA], gkv_ref[...]).astype(BF16)
    kvt = _dot(wukv_ref[...], ckvn)

    zcx = _dot_nt(h, win_ref[conv_lo + CONV_WIDTH:])
    ccx_ref[...] = (zcx[:, :CONV_WIDTH] * zcx[:, CONV_WIDTH:]).astype(BF16)

    cos = cos_ref[...]
    sin = sin_ref[...]
    gqh = gqh_ref[...]
    for hd in range(N_HEADS):
        blk = qt[hd * HEAD_PAD:(hd + 1) * HEAD_PAD]
        qn = _rms_cols(blk[:D_NOPE], gqh[:D_NOPE])
        qr = _rms_cols(blk[D_NOPE:D_QK], gqh[D_NOPE:])
        r1, r2 = _rope_cols(qr, cos, sin)
        full = jnp.concatenate([qn, r1, r2, blk[D_QK:]], axis=0)
        full = full.astype(BF16)
        for j in range(Q_PER_ROW_TILE):
            qt_ref[j, hd] = full[:, j * Q_TILE:(j + 1) * Q_TILE]

    for hd in range(N_HEADS):
        lo = N_HEADS * D_NOPE + hd * D_V
        vt_ref[hd, 0:D_V, :] = kvt[lo:lo + D_V].astype(BF16)
        vt_ref[hd, D_V:, :] = jnp.ones((V_ROWS - D_V, kvt.shape[1]), BF16)

    gkh = gkh_ref[...]
    kr = _rms_cols(zt[Q_LORA + KV_LORA:N_LATENT], gkh[D_NOPE:])
    kr1, kr2 = _rope_cols(kr, cos, sin)
    pad = jnp.zeros((HEAD_PAD - D_QK, kr.shape[1]), F32)
    for hd in range(N_HEADS):
        kn = _rms_cols(kvt[hd * D_NOPE:(hd + 1) * D_NOPE], gkh[:D_NOPE])
        full = jnp.concatenate([kn, kr1, kr2, pad], axis=0)
        k_ref[hd] = full.T.astype(BF16)


def _inproj(layer, x2d, tabs, w, cast_stacks):
    n_tokens = x2d.shape[0]
    tm = ROW_TILE
    cos_t, sin_t = tabs
    cast_in, cast_out, cast_shapes = _cast_plan(layer, cast_stacks, n_tokens // tm, lambda i: i)
    row = lambda width: pl.BlockSpec((tm, width), lambda i: (i, 0))
    col = lambda height: pl.BlockSpec((height, tm), lambda i: (0, i))
    outs = [
        jax.ShapeDtypeStruct((n_tokens // Q_TILE, N_HEADS, HEAD_PAD, Q_TILE), BF16),
        jax.ShapeDtypeStruct((N_HEADS, n_tokens, HEAD_PAD), BF16),
        jax.ShapeDtypeStruct((N_HEADS, V_ROWS, n_tokens), BF16),
        jax.ShapeDtypeStruct((n_tokens, CONV_WIDTH), BF16),
        jax.ShapeDtypeStruct((n_tokens, CONV_WIDTH), BF16),
    ]
    results = pl.pallas_call(
        functools.partial(_inproj_kernel, len(cast_stacks)),
        grid=(n_tokens // tm,),
        in_specs=[row(D_MODEL),
                  _layer_spec(layer, 1, D_MODEL),
                  _layer_spec(layer, N_LATENT + N_CONV_IN, D_MODEL),
                  _layer_spec(layer, Q_LORA, 1),
                  _layer_spec(layer, N_HEADS * HEAD_PAD, Q_LORA),
                  _layer_spec(layer, KV_LORA, 1),
                  _layer_spec(layer, N_HEADS * (D_NOPE + D_V), KV_LORA),
                  _layer_spec(layer, D_QK, 1),
                  _layer_spec(layer, D_QK, 1),
                  col(HALF_ROPE), col(HALF_ROPE)] + cast_in,
        out_specs=[pl.BlockSpec((Q_PER_ROW_TILE, N_HEADS, HEAD_PAD, Q_TILE), lambda i: (i, 0, 0, 0)),
                   pl.BlockSpec((N_HEADS, tm, HEAD_PAD), lambda i: (0, i, 0)),
                   pl.BlockSpec((N_HEADS, V_ROWS, tm), lambda i: (0, 0, i)),
                   row(CONV_WIDTH), row(CONV_WIDTH)] + cast_out,
        out_shape=outs + cast_shapes,
        compiler_params=_params(1),
        name="inproj",
    )(x2d, w["g_mix"], w["w_in"], w["g_q_lat"], w["w_uq_t"], w["g_kv_lat"],
      w["w_ukv_t"], w["g_q_head"], w["g_k_head"], cos_t, sin_t, *cast_stacks)
    return results[:5], results[5:]


def _attention_kernel(n_cast, qt_ref, k_ref, vt_ref, *refs):
    cast_src, (yt_ref, *cast_dst) = refs[:n_cast], refs[n_cast:2 * n_cast + 1]
    s0_ref, s1_ref = refs[2 * n_cast + 1:]
    _cast_slabs(cast_src, cast_dst)
    nq, heads = qt_ref.shape[:2]
    s_refs = (s0_ref, s1_ref)

    seq = k_ref.shape[1]
    n_chunks = seq // KEY_CHUNK
    rows = lambda c: pl.ds(c * KEY_CHUNK, KEY_CHUNK)

    def score_chunk(q, hd, c, m):
        s = _dot(k_ref[hd, rows(c), :], q)
        s_refs[hd % 2][rows(c), :] = s
        part = jnp.max(s, axis=0, keepdims=True)
        return part if m is None else jnp.maximum(m, part)

    def value_chunk(hd, c, m, o):
        p = jnp.exp2(s_refs[hd % 2][rows(c), :] - m).astype(BF16)
        part = _dot(vt_ref[hd, :, rows(c)], p)
        return part if o is None else o + part

    def emit(qi, hd, o):
        l = o[D_V:D_V + 1]
        yt_ref[qi, hd] = (o[:D_V] * (1.0 / l)).astype(BF16)

    def scores(qi, hd):
        q = qt_ref[qi, hd]
        m = None
        for c in range(n_chunks):
            m = score_chunk(q, hd, c, m)
        return m

    def overlapped(qi_scores, hd_scores, qi_values, hd_values, m_values):
        q = qt_ref[qi_scores, hd_scores]
        m, o = None, None
        for c in range(n_chunks):
            m = score_chunk(q, hd_scores, c, m)
            o = value_chunk(hd_values, c, m_values, o)
        emit(qi_values, hd_values, o)
        return m

    def step(qi, m):
        for hd in range(heads - 1):
            m = overlapped(qi, hd + 1, qi, hd, m)
        return overlapped(qi + 1, 0, qi, heads - 1, m)

    m = jax.lax.fori_loop(0, nq - 1, step, scores(0, 0))
    for hd in range(heads - 1):
        m = overlapped(nq - 1, hd + 1, nq - 1, hd, m)
    o = None
    for c in range(n_chunks):
        o = value_chunk(heads - 1, c, m, o)
    emit(nq - 1, heads - 1, o)


def _attention(layer, qt, k, vt, cast_stacks, batch, seq):
    tq = Q_TILE
    nq = seq // tq
    n_groups = N_HEADS // HEADS_PER_STEP
    cast_in, cast_out, cast_shapes = _cast_plan(
        layer, cast_stacks, batch * n_groups, lambda b, g: b * n_groups + g)
    outs = pl.pallas_call(
        functools.partial(_attention_kernel, len(cast_stacks)),
        grid=(batch, n_groups),
        in_specs=[pl.BlockSpec((nq, HEADS_PER_STEP, HEAD_PAD, tq), lambda b, g: (b, g, 0, 0)),
                  pl.BlockSpec((HEADS_PER_STEP, seq, HEAD_PAD), lambda b, g: (g, b, 0)),
                  pl.BlockSpec((HEADS_PER_STEP, V_ROWS, seq), lambda b, g: (g, 0, b))] + cast_in,
        out_specs=[pl.BlockSpec((nq, HEADS_PER_STEP, D_V, tq), lambda b, g: (b, g, 0, 0))]
        + cast_out,
        out_shape=[jax.ShapeDtypeStruct((batch * nq, N_HEADS, D_V, tq), BF16)] + cast_shapes,
        scratch_shapes=[pltpu.VMEM((seq, tq), F32)] * 2,
        compiler_params=_params(2),
        name="attention",
    )(qt, k, vt, *cast_stacks)
    return outs[0], outs[1:]


def _conv3(buf_ref, w, n_rows):
    u = buf_ref[...]
    total = u.shape[0]
    full = (pltpu.roll(w[0:1] * u, 1, axis=0)
            + w[1:2] * u
            + pltpu.roll(w[2:3] * u, total - 1, axis=0))
    return full[HALO:HALO + n_rows]


def _edge_masks(tiles_per_seq):
    j = pl.program_id(0) % tiles_per_seq
    return (j != 0).astype(F32), (j != tiles_per_seq - 1).astype(F32)


def _merge_kernel(tiles_per_seq, x_ref, yt_ref, cb_ref, ccx_ref, ccx_prev_ref, ccx_next_ref,
                  gmix_ref, win_ref, bgate_ref, wconv_ref, wau_ref, wcu_ref, wo_ref,
                  out_ref, cs_ref):
    tm = x_ref.shape[0]
    keep_prev, keep_next = _edge_masks(tiles_per_seq)
    x = x_ref[...]
    h = _rms_rows(x, gmix_ref[...]).astype(BF16)

    two = 2 * HALO
    cs_ref[0:HALO, :] = ccx_prev_ref[...].astype(F32)[two - HALO:] * keep_prev
    cs_ref[HALO:HALO + tm, :] = ccx_ref[...].astype(F32)
    cs_ref[HALO + tm:, :] = ccx_next_ref[...].astype(F32)[:HALO] * keep_next
    y_conv = (cb_ref[...].astype(F32) * _conv3(cs_ref, wconv_ref[...], tm)).astype(BF16)

    y_attn = jnp.concatenate(
        [yt_ref[j].reshape(N_HEADS * D_V, Q_TILE).astype(F32).T for j in range(Q_PER_ROW_TILE)],
        axis=0).astype(BF16)
    gate_lo = N_LATENT + N_CONV_IN
    bgate = bgate_ref[...]
    g_attn = jax.nn.sigmoid(_dot_nt(h, win_ref[gate_lo:gate_lo + D_MODEL]) + bgate[:, :D_MODEL])
    merged = g_attn * _dot(y_attn, wau_ref[...])
    g_conv = jax.nn.sigmoid(_dot_nt(h, win_ref[gate_lo + D_MODEL:]) + bgate[:, D_MODEL:])
    merged = merged + g_conv * _dot(y_conv, wcu_ref[...])
    out_ref[...] = x + _dot(merged.astype(BF16), wo_ref[...])


def _merge(layer, x2d, yt, cb, ccx, w, branch_weights, seq):
    n_tokens = x2d.shape[0]
    tm = ROW_TILE
    two = 2 * HALO
    per_tile = tm // two
    last_blk = n_tokens // two - 1
    row = lambda width: pl.BlockSpec((tm, width), lambda i: (i, 0))
    return pl.pallas_call(
        functools.partial(_merge_kernel, seq // tm),
        grid=(n_tokens // tm,),
        in_specs=[row(D_MODEL),
                  pl.BlockSpec((Q_PER_ROW_TILE, N_HEADS, D_V, Q_TILE), lambda i: (i, 0, 0, 0)),
                  row(CONV_WIDTH), row(CONV_WIDTH),
                  pl.BlockSpec((two, CONV_WIDTH), lambda i: (jnp.maximum(i * per_tile - 1, 0), 0)),
                  pl.BlockSpec((two, CONV_WIDTH),
                               lambda i: (jnp.minimum((i + 1) * per_tile, last_blk), 0)),
                  _layer_spec(layer, 1, D_MODEL),
                  _layer_spec(layer, N_LATENT + N_CONV_IN + N_GATES, D_MODEL),
                  _layer_spec(layer, 1, N_GATES),
                  _layer_spec(layer, 3, CONV_WIDTH),
                  _whole_spec(N_HEADS * D_V, D_MODEL),
                  _whole_spec(CONV_WIDTH, D_MODEL),
                  _whole_spec(D_MODEL, D_MODEL)],
        out_specs=row(D_MODEL),
        out_shape=jax.ShapeDtypeStruct((n_tokens, D_MODEL), F32),
        scratch_shapes=[pltpu.VMEM((tm + 2 * HALO, CONV_WIDTH), F32)],
        compiler_params=_params(1),
        name="merge",
    )(x2d, yt, cb, ccx, ccx, ccx, w["g_mix"], w["w_in"], w["b_gate"], w["w_conv"],
      *branch_weights)


def _ffn_kernel(tiles_per_seq, x_ref, xprev_ref, xnext_ref, p_ref, gffn_ref, wup_ref, wfc_ref,
                wdn_ref, gple_ref, wpg_ref, wple_ref, out_ref, lhs_ref, ua0_ref, uv0_ref,
                ua1_ref, uv1_ref, act_ref):
    tm = x_ref.shape[0]
    out_ref[...] = _dot(p_ref[...].astype(BF16), wple_ref[...])
    keep_prev, keep_next = _edge_masks(tiles_per_seq)
    x = x_ref[...]
    gffn = gffn_ref[...]
    lhs_ref[...] = jnp.concatenate(
        [_rms_rows(xprev_ref[...], gffn) * keep_prev,
         _rms_rows(x, gffn),
         _rms_rows(xnext_ref[...], gffn) * keep_next], axis=0).astype(BF16)
    u_bufs = ((ua0_ref, uv0_ref), (ua1_ref, uv1_ref))

    def cols(j, half):
        return pl.ds(half * D_FF + j * FF_CHUNK, FF_CHUNK)

    def up(j):
        lhs = lhs_ref[...]
        ua_ref, uv_ref = u_bufs[j % 2]
        ua_ref[...] = _dot(lhs, wup_ref[:, cols(j, 0)])
        uv_ref[...] = _dot(lhs, wup_ref[:, cols(j, 1)])

    def gate(j):
        ua_ref, uv_ref = u_bufs[j % 2]
        a = _conv3(ua_ref, wfc_ref[:, cols(j, 0)], tm)
        v = _conv3(uv_ref, wfc_ref[:, cols(j, 1)], tm)
        act_ref[:, cols(j, 0)] = (a * jax.nn.sigmoid(a) * v).astype(BF16)

    def down(first, last):
        span = pl.ds(first * FF_CHUNK, (last - first) * FF_CHUNK)
        return _dot(act_ref[:, span], wdn_ref[span, :])

    y = None
    up(0)
    for j in range(N_FF_CHUNKS):
        if j + 1 < N_FF_CHUNKS:
            up(j + 1)
        gate(j)
        if j + 1 in DOWN_SPLITS:
            first = ([0] + list(DOWN_SPLITS))[DOWN_SPLITS.index(j + 1)]
            part = down(first, j + 1)
            y = part if y is None else y + part

    x2 = x + y
    h3 = _rms_rows(x2, gple_ref[...]).astype(BF16)
    pg = jax.nn.sigmoid(_dot(h3, wpg_ref[...]))
    out_ref[...] = x2 + pg * out_ref[...]


def _ffn(layer, x2d, p3d, w, mixer_weights, seq):
    w_up_b, w_down_b, w_ple_gate_b, w_ple_b = mixer_weights
    n_tokens = x2d.shape[0]
    tm = FFN_TILE
    per_tile = tm // HALO
    last_blk = n_tokens // HALO - 1
    row = lambda width: pl.BlockSpec((tm, width), lambda i: (i, 0))
    return pl.pallas_call(
        functools.partial(_ffn_kernel, seq // tm),
        grid=(n_tokens // tm,),
        in_specs=[row(D_MODEL),
                  pl.BlockSpec((HALO, D_MODEL), lambda i: (jnp.maximum(i * per_tile - 1, 0), 0)),
                  pl.BlockSpec((HALO, D_MODEL),
                               lambda i: (jnp.minimum((i + 1) * per_tile, last_blk), 0)),
                  pl.BlockSpec((None, tm, PLE_DIM), lambda i: (layer, i, 0)),
                  _layer_spec(layer, 1, D_MODEL),
                  _whole_spec(D_MODEL, 2 * D_FF),
                  _layer_spec(layer, 3, 2 * D_FF),
                  _whole_spec(D_FF, D_MODEL),
                  _layer_spec(layer, 1, D_MODEL),
                  _whole_spec(D_MODEL, D_MODEL),
                  _whole_spec(PLE_DIM, D_MODEL)],
        out_specs=row(D_MODEL),
        out_shape=jax.ShapeDtypeStruct((n_tokens, D_MODEL), F32),
        scratch_shapes=[pltpu.VMEM((tm + 2 * HALO, D_MODEL), BF16)]
        + [pltpu.VMEM((tm + 2 * HALO, FF_CHUNK), F32)] * 4
        + [pltpu.VMEM((tm, D_FF), BF16)],
        compiler_params=_params(1),
        name="ffn",
    )(x2d, x2d, x2d, p3d, w["g_ffn"], w_up_b, w["w_ffn_conv"], w_down_b, w["g_ple"],
      w_ple_gate_b, w_ple_b)


def _prep_weights(w_in, b_gate, g_mix, g_q_lat, w_uq, g_kv_lat, w_ukv, g_q_head, g_k_head,
                  w_conv, g_ffn, w_ffn_conv, g_ple):
    depth = w_in.shape[0]
    as_row = lambda g: g.reshape(depth, 1, g.shape[-1])
    as_col = lambda g: g.reshape(depth, g.shape[-1], 1)

    uq = w_uq.reshape(depth, Q_LORA, N_HEADS, D_QK)
    uq = jnp.pad(uq, ((0, 0), (0, 0), (0, 0), (0, HEAD_PAD - D_QK)))
    w_uq_t = uq.reshape(depth, Q_LORA, N_HEADS * HEAD_PAD).transpose(0, 2, 1).astype(BF16)

    ukv = w_ukv.reshape(depth, KV_LORA, N_HEADS, D_NOPE + D_V)
    uk = ukv[..., :D_NOPE].reshape(depth, KV_LORA, N_HEADS * D_NOPE)
    uv = ukv[..., D_NOPE:].reshape(depth, KV_LORA, N_HEADS * D_V)
    w_ukv_t = jnp.concatenate([uk, uv], axis=2).transpose(0, 2, 1).astype(BF16)

    return {
        "g_mix": as_row(g_mix),
        "w_in": w_in.transpose(0, 2, 1).astype(BF16),
        "b_gate": as_row(b_gate),
        "g_q_lat": as_col(g_q_lat),
        "w_uq_t": w_uq_t,
        "g_kv_lat": as_col(g_kv_lat),
        "w_ukv_t": w_ukv_t,
        "g_q_head": as_col(g_q_head * (D_QK ** -0.5 * LOG2_E)),
        "g_k_head": as_col(g_k_head),
        "w_conv": w_conv,
        "g_ffn": as_row(g_ffn),
        "w_ffn_conv": w_ffn_conv,
        "g_ple": as_row(g_ple),
    }


def kernel(x, p, positions, w_in, b_gate, g_mix, g_q_lat, w_uq, g_kv_lat, w_ukv, g_q_head,
           g_k_head, w_attn_up, w_conv, w_conv_up, w_o, g_ffn, w_up, w_ffn_conv, w_down,
           g_ple, w_ple_gate, w_ple):
    batch, seq, d_model = x.shape
    depth = w_in.shape[0]
    assert d_model == D_MODEL and seq % ROW_TILE == 0 and seq % FFN_TILE == 0
    n_tokens = batch * seq
    tabs = _rope_tables(positions, n_tokens, seq)
    w = _prep_weights(w_in, b_gate, g_mix, g_q_lat, w_uq, g_kv_lat, w_ukv, g_q_head, g_k_head,
                      w_conv, g_ffn, w_ffn_conv, g_ple)
    x2d = x.reshape(n_tokens, D_MODEL)
    p3d = p.reshape(depth, n_tokens, PLE_DIM)
    for layer in range(depth):
        (qt, k, vt, cb, ccx), branch_weights = _inproj(
            layer, x2d, tabs, w, (w_attn_up, w_conv_up, w_o))
        yt, mixer_weights = _attention(
            layer, qt, k, vt, (w_up, w_down, w_ple_gate, w_ple), batch, seq)
        x2d = _merge(layer, x2d, yt, cb, ccx, w, branch_weights, seq)
        x2d = _ffn(layer, x2d, p3d, w, mixer_weights, seq)
    return x2d.reshape(batch, seq, D_MODEL)
```
